```python
import jax, jax.numpy as jnp
from jax import lax
import numpy as np

D_MODEL = 1024
BATCH = 1
SEQ = 16384
DEPTH = 2
DEC_BATCH = 8
DEC_SEQ = 16
PAST_LEN = 4096

CHUNK = 64
Q_BLOCK = 128
NORM_EPS = 1e-6
D_FF = 4 * D_MODEL

GDN_HEADS = 4
GDN_DK = 128
GDN_DV = 128
GDN_CONV = 4
GDN_QKV = GDN_HEADS * (2 * GDN_DK + GDN_DV)

FOX_HEADS = 4
FOX_DH = 128

IN0_SIZES = (GDN_QKV, GDN_HEADS * GDN_DV, GDN_HEADS, GDN_HEADS,
             FOX_HEADS * FOX_DH, FOX_HEADS * FOX_DH, FOX_HEADS * FOX_DH, FOX_HEADS)
IN0_COLS = sum(IN0_SIZES)
MIX0_WIDTH = GDN_HEADS * GDN_DV + FOX_HEADS * FOX_DH

RWKV_HEAD = 64
RWKV_HEADS = D_MODEL // RWKV_HEAD
RWKV_DECAY_LORA = 64
RWKV_A_LORA = 64
RWKV_GATE_LORA = 160
RWKV_GN_EPS = 1e-5 * RWKV_HEAD

F32 = jnp.float32

kernel_name = 'hybrid_stream_gdn_fox_rwkv7_step'


def _rms(x, g):
    xf = x.astype(F32)
    y = xf * lax.rsqrt(jnp.mean(xf * xf, axis=-1, keepdims=True) + NORM_EPS)
    return (y * g.astype(F32)).astype(x.dtype)


def _l2n(x):
    return x * lax.rsqrt(jnp.sum(x * x, axis=-1, keepdims=True) + NORM_EPS)


def _ada(c, w, b):
    m = jax.nn.silu(c) @ w + b
    return [t[:, None, :] for t in jnp.split(m, 6, axis=-1)]


def _split_cols(t, sizes):
    offs, acc = [], 0
    for s in sizes[:-1]:
        acc += s
        offs.append(acc)
    return jnp.split(t, offs, axis=-1)


def _sqrelu_mlp(h, w1, w2):
    return jnp.square(jax.nn.relu(h @ w1)) @ w2


def _causal_conv(u, buf, w):
    L = u.shape[1]
    full = jnp.concatenate([buf.astype(u.dtype), u], axis=1)
    y = full[:, 0:L] * w[0]
    for i in range(1, GDN_CONV):
        y = y + full[:, i:i + L] * w[i]
    return jax.nn.silu(y), full[:, L:]


def _gdn_chunk_terms(q, k, v, beta, g):
    l = q.shape[2]
    tri = jnp.tril(jnp.ones((l, l), bool))
    strict = jnp.tril(jnp.ones((l, l), bool), -1)
    gc = jnp.cumsum(g, axis=2)
    gch = jnp.swapaxes(gc, 2, 3)
    diff = gch[..., :, None] - gch[..., None, :]
    dec = jnp.where(tri, jnp.exp(jnp.where(tri, diff, 0.0)), 0.0)
    kk = jnp.einsum('bnihd,bnjhd->bnhij', k, k)
    betah = jnp.swapaxes(beta, 2, 3)
    a_strict = jnp.where(strict, betah[..., :, None] * kk * dec, 0.0)
    rhs = jnp.concatenate([v * beta[..., None], k * (beta * jnp.exp(gc))[..., None]], axis=-1)
    rhs = jnp.swapaxes(rhs, 2, 3)
    sol = lax.linalg.triangular_solve(a_strict, rhs, left_side=True, lower=True, unit_diagonal=True)
    u_val, w_k = sol[..., :GDN_DV], sol[..., GDN_DV:]
    qk = jnp.einsum('bnihd,bnjhd->bnhij', q, k) * dec
    q_dec = jnp.swapaxes(q * jnp.exp(gc)[..., None], 2, 3)
    k_tail = jnp.swapaxes(k * jnp.exp(gc[:, :, -1:] - gc)[..., None], 2, 3)
    g_last = gc[:, :, -1]
    return u_val, w_k, qk, q_dec, k_tail, g_last


def _gdn_scan(terms, s0):
    def step(s, t):
        u_val, w_k, qk, q_dec, k_tail, g_last = t
        u = u_val - jnp.einsum('bhlk,bhkv->bhlv', w_k, s)
        o = jnp.einsum('bhlk,bhkv->bhlv', q_dec, s) + jnp.einsum('bhij,bhjv->bhiv', qk, u)
        s = s * jnp.exp(g_last)[..., None, None] + jnp.einsum('bhlk,bhlv->bhkv', k_tail, u)
        return s, o
    s, o = lax.scan(step, s0, tuple(jnp.moveaxis(t, 1, 0) for t in terms))
    return jnp.moveaxis(o, 0, 1), s


def _gdn_mixer(qkv_raw, z, b_raw, a_raw, conv_buf, s0, conv_w, a_log, dt_bias, onorm):
    b, L, _ = qkv_raw.shape
    H, dk, dv = GDN_HEADS, GDN_DK, GDN_DV
    qkv, conv_new = _causal_conv(qkv_raw, conv_buf, conv_w)
    q, k, v = jnp.split(qkv, [H * dk, 2 * H * dk], axis=-1)
    q = _l2n(q.reshape(b, L, H, dk).astype(F32)) * dk ** -0.5
    k = _l2n(k.reshape(b, L, H, dk).astype(F32))
    v = v.reshape(b, L, H, dv).astype(F32)
    beta = jax.nn.sigmoid(b_raw.astype(F32))
    g = -jnp.exp(a_log.astype(F32)) * jax.nn.softplus(a_raw.astype(F32) + dt_bias.astype(F32))
    cl = min(L, CHUNK)
    n = L // cl
    blk = lambda t: t.reshape((b, n, cl) + t.shape[2:])
    terms = _gdn_chunk_terms(blk(q), blk(k), blk(v), blk(beta), blk(g))
    o, s_new = _gdn_scan(terms, s0.astype(F32))
    o = jnp.transpose(o, (0, 1, 3, 2, 4)).reshape(b, L, H, dv)
    o = _rms(o, onorm) * jax.nn.silu(z.reshape(b, L, H, dv).astype(F32))
    return o.reshape(b, L, H * dv).astype(z.dtype), conv_new, s_new


def _fox_prompt(q, k, v, logf):
    b, t, h, dh = q.shape
    nb = t // Q_BLOCK
    c = jnp.swapaxes(jnp.cumsum(logf, axis=1), 1, 2)
    qb = jnp.moveaxis(q.reshape(b, nb, Q_BLOCK, h, dh), 1, 0)
    cb = jnp.moveaxis(c.reshape(b, h, nb, Q_BLOCK), 2, 0)
    key_pos = jnp.arange(t)

    def block(args):
        qi, ci, i = args
        s = jnp.einsum('bqhd,bkhd->bhqk', qi, k, preferred_element_type=F32)
        s = s + ci[..., :, None] - c[..., None, :]
        qpos = i * Q_BLOCK + jnp.arange(Q_BLOCK)
        s = jnp.where(key_pos[None, :] <= qpos[:, None], s, -jnp.inf)
        p = jax.nn.softmax(s, axis=-1)
        return jnp.einsum('bhqk,bkhd->bqhd', p.astype(v.dtype), v)

    o = lax.map(block, (qb, cb, jnp.arange(nb)))
    return jnp.moveaxis(o, 0, 1).reshape(b, t, h, dh)


def _fox_step(q, k_new, v_new, logf_new, k_cache, v_cache, logf_cache):
    p_len = k_cache.shape[1]
    l = q.shape[1]
    k = jnp.concatenate([k_cache.astype(k_new.dtype), k_new], axis=1)
    v = jnp.concatenate([v_cache.astype(v_new.dtype), v_new], axis=1)
    c = jnp.cumsum(jnp.concatenate([logf_cache.astype(F32), logf_new], axis=1), axis=1)
    c = jnp.swapaxes(c, 1, 2)
    s = jnp.einsum('bqhd,bkhd->bhqk', q, k, preferred_element_type=F32)
    s = s + c[..., p_len:, None] - c[..., None, :]
    mask = jnp.arange(p_len + l)[None, :] <= (p_len + jnp.arange(l))[:, None]
    s = jnp.where(mask, s, -jnp.inf)
    p = jax.nn.softmax(s, axis=-1)
    return jnp.einsum('bhqk,bkhd->bqhd', p.astype(v.dtype), v)


def _hybrid_layer(x, c, conv_buf, s0, fox_cache, ada_w, ada_b, norm_mix, norm_ff, w_in, conv_w,
                  a_log, dt_bias, gdn_onorm, fox_qnorm, fox_knorm, fox_fbias, w_out, ff_w1, ff_w2):
    b, L, _ = x.shape
    sh1, sc1, gt1, sh2, sc2, gt2 = _ada(c, ada_w, ada_b)
    h = _rms(x, norm_mix) * (1 + sc1) + sh1
    proj = h @ w_in
    qkv_a, z_a, b_a, a_a, q_b, k_b, v_b, f_b = _split_cols(proj, IN0_SIZES)
    o_a, conv_new, s_new = _gdn_mixer(qkv_a, z_a, b_a, a_a, conv_buf, s0, conv_w, a_log, dt_bias, gdn_onorm)
    q = _rms(q_b.reshape(b, L, FOX_HEADS, FOX_DH), fox_qnorm) * FOX_DH ** -0.5
    k = _rms(k_b.reshape(b, L, FOX_HEADS, FOX_DH), fox_knorm)
    v = v_b.reshape(b, L, FOX_HEADS, FOX_DH)
    logf = jax.nn.log_sigmoid(f_b.astype(F32) + fox_fbias.astype(F32))
    if fox_cache is None:
        o_b = _fox_prompt(q, k, v, logf)
    else:
        o_b = _fox_step(q, k, v, logf, *fox_cache)
    mix = jnp.concatenate([o_a, o_b.reshape(b, L, FOX_HEADS * FOX_DH).astype(o_a.dtype)], axis=-1) @ w_out
    x = x + gt1 * mix
    h2 = _rms(x, norm_ff) * (1 + sc2) + sh2
    x = x + gt2 * _sqrelu_mlp(h2, ff_w1, ff_w2)
    return x, conv_new, s_new, k, v, logf


def _wkv7_scan(r, decay, k, v, kk, bvec, s0):
    def step(s, t):
        r_t, w_t, k_t, v_t, kk_t, b_t = t
        sa = -jnp.einsum('bhij,bhj->bhi', s, kk_t)
        s = (s * w_t[:, :, None, :] + sa[..., None] * b_t[:, :, None, :]
             + v_t[..., None] * k_t[:, :, None, :])
        return s, jnp.einsum('bhij,bhj->bhi', s, r_t)
    xs = tuple(jnp.moveaxis(t, 1, 0) for t in (r, decay, k, v, kk, bvec))
    s, y = lax.scan(step, s0, xs)
    return jnp.moveaxis(y, 0, 1), s


def _rwkv_layer(x, c, shift_prev, s0, ada_w, ada_b, norm_mix, norm_ff, mu, w_r, w_k, w_v,
                w0, w1, w2, a0, a1, a2, g1, g2, k_k, k_a, r_k, ln_w, ln_b, w_o, ff_w1, ff_w2):
    b, L, D = x.shape
    H, N = RWKV_HEADS, RWKV_HEAD
    sh1, sc1, gt1, sh2, sc2, gt2 = _ada(c, ada_w, ada_b)
    h = _rms(x, norm_mix) * (1 + sc1) + sh1
    h_prev = jnp.concatenate([shift_prev[:, None, :].astype(h.dtype), h[:, :-1]], axis=1)
    xx = h_prev - h
    xr, xw, xk, xv, xa, xg = [h + xx * mu[i] for i in range(6)]
    r = xr @ w_r
    k = xk @ w_k
    v = xv @ w_v
    w = -jax.nn.softplus(-(w0 + jnp.tanh(xw @ w1) @ w2).astype(F32)) - 0.5
    a = jax.nn.sigmoid((a0 + (xa @ a1) @ a2).astype(F32))
    gate = jax.nn.sigmoid(xg @ g1) @ g2
    heads = lambda t: t.astype(F32).reshape(b, L, H, N)
    kk = _l2n(heads(k * k_k))
    k = k.astype(F32) * (1 + (a - 1) * k_a.astype(F32))
    r_h, k_h, v_h, a_h = heads(r), heads(k), heads(v), heads(a)
    decay = jnp.exp(-jnp.exp(heads(w)))
    y, s_new = _wkv7_scan(r_h, decay, k_h, v_h, kk, kk * a_h, s0.astype(F32))
    mean = jnp.mean(y, axis=-1, keepdims=True)
    var = jnp.mean(jnp.square(y - mean), axis=-1, keepdims=True)
    y = ((y - mean) * lax.rsqrt(var + RWKV_GN_EPS)).reshape(b, L, D) * ln_w.astype(F32) + ln_b.astype(F32)
    bonus = jnp.sum(r_h * k_h * r_k.astype(F32), axis=-1, keepdims=True) * v_h
    y = (y + bonus.reshape(b, L, D)).astype(x.dtype)
    x = x + gt1 * ((y * gate) @ w_o)
    h2 = _rms(x, norm_ff) * (1 + sc2) + sh2
    x = x + gt2 * _sqrelu_mlp(h2, ff_w1, ff_w2)
    return x, h[:, -1], s_new


def setup_inputs(seed: int = 0) -> dict:
    key = jax.random.key(seed)
    ks = iter(jax.random.split(key, 64))
    D = D_MODEL

    def nrm(shape, s=1.0):
        return s * jax.random.normal(next(ks), shape, F32)

    def unif(shape, lo, hi):
        return jax.random.uniform(next(ks), shape, F32, minval=lo, maxval=hi)

    def gain(n):
        return 1.0 + nrm((n,), 0.02)

    inp = {}
    inp['x_prompt'] = nrm((BATCH, SEQ, D))
    inp['x_sample'] = nrm((DEC_BATCH, DEC_SEQ, D))
    inp['c_prompt'] = nrm((BATCH, D))
    inp['c_sample'] = nrm((DEC_BATCH, D))
    inp['cache_l0_conv'] = nrm((DEC_BATCH, GDN_CONV - 1, GDN_QKV))
    inp['state_l0_delta'] = nrm((DEC_BATCH, GDN_HEADS, GDN_DK, GDN_DV), 0.5)
    inp['cache_l0_fox_k'] = nrm((DEC_BATCH, PAST_LEN, FOX_HEADS, FOX_DH))
    inp['cache_l0_fox_v'] = nrm((DEC_BATCH, PAST_LEN, FOX_HEADS, FOX_DH))
    inp['cache_l0_fox_logf'] = jax.nn.log_sigmoid(2.5 + nrm((DEC_BATCH, PAST_LEN, FOX_HEADS)))
    inp['state_l1_shift'] = nrm((DEC_BATCH, D))
    inp['state_l1_wkv'] = nrm((DEC_BATCH, RWKV_HEADS, RWKV_HEAD, RWKV_HEAD), 0.3)
    inp['l0_ada_w'] = nrm((D, 6 * D), 0.3 * D ** -0.5)
    inp['l0_ada_b'] = nrm((6 * D,), 0.02)
    inp['l0_norm_mix'] = gain(D)
    inp['l0_norm_ff'] = gain(D)
    inp['l0_w_in'] = nrm((D, IN0_COLS), D ** -0.5)
    inp['l0_conv_w'] = nrm((GDN_CONV, GDN_QKV), GDN_CONV ** -0.5)
    inp['l0_a_log'] = jnp.log(unif((GDN_HEADS,), 1.0, 16.0))
    dt = jnp.exp(unif((GDN_HEADS,), float(np.log(1e-3)), float(np.log(1e-1))))
    inp['l0_dt_bias'] = jnp.log(jnp.expm1(dt))
    inp['l0_gdn_onorm'] = gain(GDN_DV)
    inp['l0_fox_qnorm'] = gain(FOX_DH)
    inp['l0_fox_knorm'] = gain(FOX_DH)
    inp['l0_fox_fbias'] = 2.5 + nrm((FOX_HEADS,), 0.1)
    inp['l0_w_out'] = nrm((MIX0_WIDTH, D), MIX0_WIDTH ** -0.5)
    inp['l0_ff_w1'] = nrm((D, D_FF), D ** -0.5)
    inp['l0_ff_w2'] = nrm((D_FF, D), D_FF ** -0.5)
    inp['l1_ada_w'] = nrm((D, 6 * D), 0.3 * D ** -0.5)
    inp['l1_ada_b'] = nrm((6 * D,), 0.02)
    inp['l1_norm_mix'] = gain(D)
    inp['l1_norm_ff'] = gain(D)
    inp['l1_mu'] = unif((6, D), 0.0, 1.0)
    inp['l1_w_r'] = nrm((D, D), D ** -0.5)
    inp['l1_w_k'] = nrm((D, D), D ** -0.5)
    inp['l1_w_v'] = nrm((D, D), D ** -0.5)
    inp['l1_w0'] = unif((D,), -6.0, -1.0)
    inp['l1_w1'] = nrm((D, RWKV_DECAY_LORA), D ** -0.5)
    inp['l1_w2'] = nrm((RWKV_DECAY_LORA, D), 0.1 * RWKV_DECAY_LORA ** -0.5)
    inp['l1_a0'] = nrm((D,), 0.1)
    inp['l1_a1'] = nrm((D, RWKV_A_LORA), D ** -0.5)
    inp['l1_a2'] = nrm((RWKV_A_LORA, D), 0.1 * RWKV_A_LORA ** -0.5)
    inp['l1_g1'] = nrm((D, RWKV_GATE_LORA), D ** -0.5)
    inp['l1_g2'] = nrm((RWKV_GATE_LORA, D), RWKV_GATE_LORA ** -0.5)
    inp['l1_k_k'] = 0.85 + nrm((D,), 0.02)
    inp['l1_k_a'] = gain(D)
    inp['l1_r_k'] = nrm((RWKV_HEADS, RWKV_HEAD), 0.1)
    inp['l1_ln_w'] = gain(D)
    inp['l1_ln_b'] = nrm((D,), 0.01)
    inp['l1_w_o'] = nrm((D, D), D ** -0.5)
    inp['l1_ff_w1'] = nrm((D, D_FF), D ** -0.5)
    inp['l1_ff_w2'] = nrm((D_FF, D), D_FF ** -0.5)
    return inp


def reference(x_prompt, x_sample, c_prompt, c_sample, cache_l0_conv, state_l0_delta,
              cache_l0_fox_k, cache_l0_fox_v, cache_l0_fox_logf, state_l1_shift, state_l1_wkv,
              l0_ada_w, l0_ada_b, l0_norm_mix, l0_norm_ff, l0_w_in, l0_conv_w, l0_a_log, l0_dt_bias,
              l0_gdn_onorm, l0_fox_qnorm, l0_fox_knorm, l0_fox_fbias, l0_w_out, l0_ff_w1, l0_ff_w2,
              l1_ada_w, l1_ada_b, l1_norm_mix, l1_norm_ff, l1_mu, l1_w_r, l1_w_k, l1_w_v,
              l1_w0, l1_w1, l1_w2, l1_a0, l1_a1, l1_a2, l1_g1, l1_g2, l1_k_k, l1_k_a, l1_r_k,
              l1_ln_w, l1_ln_b, l1_w_o, l1_ff_w1, l1_ff_w2):
    even_layers = [(l0_ada_w, l0_ada_b, l0_norm_mix, l0_norm_ff, l0_w_in, l0_conv_w, l0_a_log, l0_dt_bias,
                    l0_gdn_onorm, l0_fox_qnorm, l0_fox_knorm, l0_fox_fbias, l0_w_out, l0_ff_w1, l0_ff_w2)]
    odd_layers = [(l1_ada_w, l1_ada_b, l1_norm_mix, l1_norm_ff, l1_mu, l1_w_r, l1_w_k, l1_w_v,
                   l1_w0, l1_w1, l1_w2, l1_a0, l1_a1, l1_a2, l1_g1, l1_g2, l1_k_k, l1_k_a, l1_r_k,
                   l1_ln_w, l1_ln_b, l1_w_o, l1_ff_w1, l1_ff_w2)]
    y_prompt, y_sample = x_prompt, x_sample
    bp = x_prompt.shape[0]
    for layer in range(DEPTH):
        if layer % 2 == 0:
            prm = even_layers[layer // 2]
            y_prompt, conv_p, delta_p, fox_k_p, fox_v_p, fox_logf_p = _hybrid_layer(
                y_prompt, c_prompt,
                jnp.zeros((bp, GDN_CONV - 1, GDN_QKV), y_prompt.dtype),
                jnp.zeros((bp, GDN_HEADS, GDN_DK, GDN_DV), F32),
                None, *prm)
            y_sample, conv_s, delta_s, fox_k_s, fox_v_s, fox_logf_s = _hybrid_layer(
                y_sample, c_sample, cache_l0_conv, state_l0_delta,
                (cache_l0_fox_k, cache_l0_fox_v, cache_l0_fox_logf), *prm)
        else:
            prm = odd_layers[layer // 2]
            y_prompt, shift_p, wkv_p = _rwkv_layer(
                y_prompt, c_prompt, jnp.zeros((bp, D_MODEL), y_prompt.dtype),
                jnp.zeros((bp, RWKV_HEADS, RWKV_HEAD, RWKV_HEAD), F32), *prm)
            y_sample, shift_s, wkv_s = _rwkv_layer(
                y_sample, c_sample, state_l1_shift, state_l1_wkv, *prm)
    return (y_prompt, y_sample, conv_p, conv_s, delta_p, delta_s, fox_k_p, fox_k_s,
            fox_v_p, fox_v_s, fox_logf_p, fox_logf_s, shift_p, shift_s, wkv_p, wkv_s)
```

```python
import functools
import math

import jax
import jax.numpy as jnp
from jax import lax
from jax.experimental import pallas as pl
from jax.experimental.pallas import tpu as pltpu

F32 = jnp.float32
BF16 = jnp.bfloat16

D_MODEL = 1024
D_FF = 4 * D_MODEL
NORM_EPS = 1e-6

GDN_HEADS = 4
GDN_DK = 128
GDN_DV = 128
GDN_CONV = 4
GDN_QKV = GDN_HEADS * (2 * GDN_DK + GDN_DV)
GDN_CHUNK = 64

FOX_HEADS = 4
FOX_DH = 128
FOX_W = FOX_HEADS * FOX_DH

RWKV_HEAD = 64
RWKV_HEADS = D_MODEL // RWKV_HEAD
RWKV_GN_EPS = 1e-5 * RWKV_HEAD
RWKV_CHUNK = 64

LANES = 128
SUBLANES = 8
VMEM_LIMIT = 56 * 1024 * 1024

NN = (((1,), (0,)), ((), ()))
NT = (((1,), (1,)), ((), ()))
TN = (((0,), (0,)), ((), ()))


def _dot(a, b, dims=NN):
    return lax.dot_general(a, b, dims, preferred_element_type=F32)


def _mm(a, b, dims=NN):
    return _dot(a.astype(BF16), b.astype(BF16), dims)


def _split2(x):
    hi = x.astype(BF16)
    lo = (x - hi.astype(F32)).astype(BF16)
    return hi, lo


def _split3(x):
    hi = x.astype(BF16)
    r = x - hi.astype(F32)
    mid = r.astype(BF16)
    lo = (r - mid.astype(F32)).astype(BF16)
    return hi, mid, lo


def _mm3(a, b, dims=NN):
    ah, al = _split2(a)
    bh, bl = _split2(b)
    return _dot(ah, bh, dims) + (_dot(ah, bl, dims) + _dot(al, bh, dims))


def _mask_mm(mask01, x):
    x1, x2, x3 = _split3(x)
    return _dot(mask01, x1) + (_dot(mask01, x2) + _dot(mask01, x3))


def _sigmoid(x):
    return 1.0 / (1.0 + jnp.exp(-x))


def _softplus(x):
    return jnp.maximum(x, 0.0) + jnp.log(1.0 + jnp.exp(-jnp.abs(x)))


def _silu(x):
    return x * _sigmoid(x)


def _rms_mod(x, g, sc, sh):
    ms = jnp.mean(x * x, axis=-1, keepdims=True)
    return (x * lax.rsqrt(ms + NORM_EPS) * g) * (1.0 + sc) + sh


def _unit_lower_solve(n, x, mm):
    size = n.shape[0]
    levels = max(1, int(math.ceil(math.log2(size))))
    for lvl in range(levels):
        x = x + mm(n, x)
        if lvl + 1 < levels:
            n = mm(n, n)
    return x


def _tri_masks(c):
    row = lax.broadcasted_iota(jnp.int32, (c, c), 0)
    col = lax.broadcasted_iota(jnp.int32, (c, c), 1)
    return row >= col, row > col, row == col


def _cparams(sem):
    return pltpu.CompilerParams(dimension_semantics=sem, vmem_limit_bytes=VMEM_LIMIT)


def _const_spec(shape):
    nd = len(shape)
    return pl.BlockSpec(shape, lambda *_: (0,) * nd, pipeline_mode=pl.Buffered(1))


def _ada_body(c_ref, w_ref, b_ref, o_ref):
    o_ref[...] = _mm3(_silu(c_ref[...]), w_ref[...]) + b_ref[...]


def _ada(c_all, w, b):
    rows, d = c_all.shape
    n = w.shape[1]
    tn = 512
    return pl.pallas_call(
        _ada_body,
        grid=(n // tn,),
        in_specs=[pl.BlockSpec((rows, d), lambda j: (0, 0)),
                  pl.BlockSpec((d, tn), lambda j: (0, j)),
                  pl.BlockSpec((1, tn), lambda j: (0, j))],
        out_specs=pl.BlockSpec((rows, tn), lambda j: (0, j)),
        out_shape=jax.ShapeDtypeStruct((rows, n), F32),
        compiler_params=_cparams(("arbitrary",)),
        name="ada_ln",
    )(c_all, w, b.reshape(1, n))


_IN0_QKV = (0, GDN_QKV)
_IN0_Z = (GDN_QKV, GDN_QKV + 512)
_IN0_Q = (2048, 2560)
_IN0_K = (2560, 3072)
_IN0_V = (3072, 3584)
_IN0_S = (3584, 3712)
_IN0_COLS = 3712


def _inproj0_body(x_ref, sh_ref, sc_ref, g_ref, w_ref, qn_ref, kn_ref, aux_ref,
                  qkv_ref, z_ref, small_ref, qb_ref, kf_ref, vf_ref, kb_ref, vb_ref):
    h = _rms_mod(x_ref[...], g_ref[...], sc_ref[...], sh_ref[...]).astype(BF16)
    qkv_ref[...] = _dot(h, w_ref[:, _IN0_QKV[0]:_IN0_QKV[1]])
    z_ref[...] = _dot(h, w_ref[:, _IN0_Z[0]:_IN0_Z[1]])
    q = _dot(h, w_ref[:, _IN0_Q[0]:_IN0_Q[1]])
    k = _dot(h, w_ref[:, _IN0_K[0]:_IN0_K[1]])
    v = _dot(h, w_ref[:, _IN0_V[0]:_IN0_V[1]])
    raw = _dot(h, w_ref[:, _IN0_S[0]:_IN0_S[1]])
    qn = qn_ref[...]
    kn = kn_ref[...]
    for hd in range(FOX_HEADS):
        sl = slice(hd * FOX_DH, (hd + 1) * FOX_DH)
        qh = q[:, sl]
        kh = k[:, sl]
        qh = qh * lax.rsqrt(jnp.mean(qh * qh, axis=-1, keepdims=True) + NORM_EPS) * qn * FOX_DH ** -0.5
        kh = kh * lax.rsqrt(jnp.mean(kh * kh, axis=-1, keepdims=True) + NORM_EPS) * kn
        qb_ref[:, sl] = qh.astype(BF16)
        kf_ref[:, sl] = kh
        kb_ref[:, sl] = kh.astype(BF16)
    vf_ref[...] = v
    vb_ref[...] = v.astype(BF16)
    aux = aux_ref[...]
    a_log, dt_bias, f_bias = aux[0:1], aux[1:2], aux[2:3]
    lane = lax.broadcasted_iota(jnp.int32, raw.shape, 1)
    beta = _sigmoid(raw)
    gdec = -jnp.exp(a_log) * _softplus(raw + dt_bias)
    logf = -_softplus(-(raw + f_bias))
    small_ref[...] = jnp.where(lane < 4, beta, jnp.where(lane < 8, gdec, jnp.where(lane < 12, logf, 0.0)))


def _inproj0(x, sh, sc, g, w_all, qn, kn, aux, tm):
    b, l, d = x.shape
    row = lambda width: pl.BlockSpec((None, tm, width), lambda bi, i: (bi, i, 0))
    mod = pl.BlockSpec((None, 1, d), lambda bi, i: (bi, 0, 0))
    out = lambda width, dt: jax.ShapeDtypeStruct((b, l, width), dt)
    return pl.pallas_call(
        _inproj0_body,
        grid=(b, l // tm),
        in_specs=[row(d), mod, mod, _const_spec((1, d)), _const_spec((d, _IN0_COLS)),
                  _const_spec((1, FOX_DH)), _const_spec((1, FOX_DH)), _const_spec((SUBLANES, LANES))],
        out_specs=[row(GDN_QKV), row(512), row(LANES), row(FOX_W), row(FOX_W), row(FOX_W), row(FOX_W), row(FOX_W)],
        out_shape=[out(GDN_QKV, F32), out(512, F32), out(LANES, F32), out(FOX_W, BF16),
                   out(FOX_W, F32), out(FOX_W, F32), out(FOX_W, BF16), out(FOX_W, BF16)],
        compiler_params=_cparams(("arbitrary", "arbitrary")),
        name="l0_in_proj",
    )(x, sh, sc, g, w_all, qn, kn, aux)


def _gdn_body(qkv_ref, z_ref, small_ref, cinit_ref, s0_ref, cw_ref, on_ref,
              o_ref, sout_ref, ext, state, *, chunk):
    c = chunk
    ci = pl.program_id(1)

    @pl.when(ci == 0)
    def _():
        ext[0:SUBLANES, :] = cinit_ref[...]
        state[...] = s0_ref[...]

    ext[SUBLANES:SUBLANES + c, :] = qkv_ref[...]
    cw = cw_ref[...]
    base = SUBLANES - (GDN_CONV - 1)
    y = ext[base:base + c, :] * cw[0:1]
    for i in range(1, GDN_CONV):
        y = y + ext[base + i:base + i + c, :] * cw[i:i + 1]
    qkv = _silu(y)
    ext[0:SUBLANES, :] = ext[c:c + SUBLANES, :]

    tri, strict, eye = _tri_masks(c)
    sm = small_ref[...]
    gc_all = _mask_mm(tri.astype(BF16), sm)
    onorm = on_ref[...]
    for hd in range(GDN_HEADS):
        q = qkv[:, hd * GDN_DK:(hd + 1) * GDN_DK]
        k = qkv[:, 512 + hd * GDN_DK:512 + (hd + 1) * GDN_DK]
        v = qkv[:, 1024 + hd * GDN_DV:1024 + (hd + 1) * GDN_DV]
        q = q * lax.rsqrt(jnp.sum(q * q, axis=-1, keepdims=True) + NORM_EPS) * GDN_DK ** -0.5
        k = k * lax.rsqrt(jnp.sum(k * k, axis=-1, keepdims=True) + NORM_EPS)
        beta = sm[:, hd:hd + 1]
        gc = gc_all[:, 4 + hd:5 + hd]
        gc_row = jnp.sum(jnp.where(eye, gc, 0.0), axis=0, keepdims=True)
        dec = jnp.where(tri, jnp.exp(jnp.where(tri, gc - gc_row, 0.0)), 0.0)
        kk = _mm3(k, k, NT)
        a_strict = jnp.where(strict, beta * kk * dec, 0.0)
        egc = jnp.exp(gc)
        rhs = jnp.concatenate([v * beta, k * (beta * egc)], axis=1)
        sol = _unit_lower_solve(-a_strict, rhs, _mm3)
        u_val, w_k = sol[:, :GDN_DV], sol[:, GDN_DV:]
        qk = _mm3(q, k, NT) * dec
        g_last = gc[c - 1:c, :]
        s_h = state[hd]
        u = u_val - _mm3(w_k, s_h)
        o = _mm3(q * egc, s_h) + _mm3(qk, u)
        state[hd] = s_h * jnp.exp(g_last) + _mm3(k * jnp.exp(g_last - gc), u, TN)
        zz = z_ref[:, hd * GDN_DV:(hd + 1) * GDN_DV]
        o = o * lax.rsqrt(jnp.mean(o * o, axis=-1, keepdims=True) + NORM_EPS) * onorm
        o_ref[:, hd * GDN_DV:(hd + 1) * GDN_DV] = (o * _silu(zz)).astype(BF16)

    @pl.when(ci == pl.num_programs(1) - 1)
    def _():
        sout_ref[...] = state[...]


def _gdn(qkv_raw, z, small, conv_init8, s0, conv_w8, onorm, chunk):
    b, l, _ = qkv_raw.shape
    row = lambda width: pl.BlockSpec((None, chunk, width), lambda bi, i: (bi, i, 0))
    st = pl.BlockSpec((None, GDN_HEADS, GDN_DK, GDN_DV), lambda bi, i: (bi, 0, 0, 0))
    return pl.pallas_call(
        functools.partial(_gdn_body, chunk=chunk),
        grid=(b, l // chunk),
        in_specs=[row(GDN_QKV), row(512), row(LANES),
                  pl.BlockSpec((None, SUBLANES, GDN_QKV), lambda bi, i: (bi, 0, 0)), st,
                  _const_spec((SUBLANES, GDN_QKV)), _const_spec((1, GDN_DV))],
        out_specs=[row(512), st],
        out_shape=[jax.ShapeDtypeStruct((b, l, 512), BF16),
                   jax.ShapeDtypeStruct((b, GDN_HEADS, GDN_DK, GDN_DV), F32)],
        scratch_shapes=[pltpu.VMEM((chunk + SUBLANES, GDN_QKV), F32),
                        pltpu.VMEM((GDN_HEADS, GDN_DK, GDN_DV), F32)],
        compiler_params=_cparams(("arbitrary", "arbitrary")),
        name="gdn_mixer",
    )(qkv_raw, z, small, conv_init8, s0, conv_w8, onorm)


def _cumsum_body(x_ref, o_ref):
    x = x_ref[...]
    rows = x.shape[0]
    li = lax.broadcasted_iota(jnp.int32, (LANES, LANES), 0)
    lj = lax.broadcasted_iota(jnp.int32, (LANES, LANES), 1)
    upper = (li <= lj).astype(BF16)
    ones = jnp.ones((LANES, LANES), BF16)
    ri = lax.broadcasted_iota(jnp.int32, (rows, rows), 0)
    rj = lax.broadcasted_iota(jnp.int32, (rows, rows), 1)
    before = (rj < ri).astype(BF16)
    x1, x2, x3 = _split3(x)
    within = _dot(x1, upper) + (_dot(x2, upper) + _dot(x3, upper))
    total = _dot(x1, ones) + (_dot(x2, ones) + _dot(x3, ones))
    o_ref[...] = within + _mask_mm(before, total)


def _cumsum_rows(x):
    g, rows, _ = x.shape
    spec = pl.BlockSpec((None, rows, LANES), lambda i: (i, 0, 0))
    return pl.pallas_call(
        _cumsum_body, grid=(g,), in_specs=[spec], out_specs=spec,
        out_shape=jax.ShapeDtypeStruct(x.shape, F32),
        compiler_params=_cparams(("arbitrary",)),
        name="logf_cumsum",
    )(x)


def _fox_prompt_body(q_ref, k_ref, v_ref, c_ref, o_ref, *, blk):
    qi = pl.program_id(1)
    q = q_ref[...]
    c0 = c_ref[qi][:, 0:1]

    def scores(j):
        kj = k_ref[pl.ds(pl.multiple_of(j * blk, blk), blk), :]
        return _dot(q, kj, NT) + (c0 - c_ref[j])

    def update(j, s, carry):
        m, l, acc = carry
        vj = v_ref[pl.ds(pl.multiple_of(j * blk, blk), blk), :]
        m_new = jnp.maximum(m, jnp.max(s, axis=-1, keepdims=True))
        alpha = jnp.exp(m - m_new)
        p = jnp.exp(s - m_new)
        l = alpha * l + jnp.sum(p, axis=-1, keepdims=True)
        acc = alpha * acc + _dot(p.astype(BF16), vj)
        return m_new, l, acc

    init = (jnp.full((blk, 1), -jnp.inf, F32), jnp.zeros((blk, 1), F32), jnp.zeros((blk, FOX_DH), F32))
    carry = lax.fori_loop(0, qi, lambda j, cr: update(j, scores(j), cr), init)
    tri, _, _ = _tri_masks(blk)
    m, l, acc = update(qi, jnp.where(tri, scores(qi), -jnp.inf), carry)
    o_ref[...] = (acc / l).astype(BF16)


def _fox_prompt(q, k, v, c_rows, blk):
    l = q.shape[0]
    nb = l // blk
    return pl.pallas_call(
        functools.partial(_fox_prompt_body, blk=blk),
        grid=(FOX_HEADS, nb),
        in_specs=[pl.BlockSpec((blk, FOX_DH), lambda h, i: (i, h)),
                  pl.BlockSpec((l, FOX_DH), lambda h, i: (0, h)),
                  pl.BlockSpec((l, FOX_DH), lambda h, i: (0, h)),
                  pl.BlockSpec((None, nb, 1, blk), lambda h, i: (h, 0, 0, 0))],
        out_specs=pl.BlockSpec((blk, FOX_DH), lambda h, i: (i, h)),
        out_shape=jax.ShapeDtypeStruct((l, FOX_W), BF16),
        compiler_params=_cparams(("arbitrary", "arbitrary")),
        name="fox_prompt_attn",
    )(q, k, v, c_rows)


def _fox_step_body(q_ref, kn_ref, vn_ref, kc_ref, vc_ref, cc_ref, cnr_ref, cnc_ref, o_ref):
    q = q_ref[...]
    n = q.shape[0]
    c_i = cnc_ref[...]
    s_c = _mm(q, kc_ref[...], NT) + (c_i - cc_ref[...])
    s_n = _mm(q, kn_ref[...], NT) + (c_i - cnr_ref[...])
    tri, _, _ = _tri_masks(n)
    s_n = jnp.where(tri, s_n, -jnp.inf)
    m = jnp.maximum(jnp.max(s_c, axis=-1, keepdims=True), jnp.max(s_n, axis=-1, keepdims=True))
    p_c = jnp.exp(s_c - m)
    p_n = jnp.exp(s_n - m)
    l = jnp.sum(p_c, axis=-1, keepdims=True) + jnp.sum(p_n, axis=-1, keepdims=True)
    o = _mm(p_c, vc_ref[...]) + _mm(p_n, vn_ref[...])
    o_ref[...] = (o / l).astype(BF16)


def _fox_step(q, k_new, v_new, k_cache, v_cache, c_cache, c_new_row, c_new_col):
    b, n, _ = q.shape
    p = k_cache.shape[1]
    new = pl.BlockSpec((None, n, FOX_DH), lambda bi, h: (bi, 0, h))
    cache = pl.BlockSpec((None, p, FOX_DH), lambda bi, h: (bi, 0, h))
    return pl.pallas_call(
        _fox_step_body,
        grid=(b, FOX_HEADS),
        in_specs=[new, new, new, cache, cache,
                  pl.BlockSpec((None, None, 1, p), lambda bi, h: (bi, h, 0, 0)),
                  pl.BlockSpec((None, None, 1, n), lambda bi, h: (bi, h, 0, 0)),
                  pl.BlockSpec((None, None, n, 1), lambda bi, h: (bi, h, 0, 0))],
        out_specs=new,
        out_shape=jax.ShapeDtypeStruct((b, n, FOX_W), BF16),
        compiler_params=_cparams(("arbitrary", "arbitrary")),
        name="fox_step_attn",
    )(q, k_new, v_new, k_cache, v_cache, c_cache, c_new_row, c_new_col)


def _resid_mlp(x, mix, gt1, sh2, sc2, gt2, g2, w1_ref, w2_ref):
    x1 = x + gt1 * mix
    h2 = _rms_mod(x1, g2, sc2, sh2).astype(BF16)
    acc = jnp.zeros_like(x1)
    step = D_MODEL
    for c0 in range(0, D_FF, step):
        hid = jnp.maximum(_dot(h2, w1_ref[:, c0:c0 + step]), 0.0)
        acc = acc + _dot((hid * hid).astype(BF16), w2_ref[c0:c0 + step, :])
    return x1 + gt2 * acc


def _mix0_mlp_body(x_ref, oa_ref, ob_ref, gt1, sh2, sc2, gt2, g2, wo_ref, w1_ref, w2_ref, out_ref):
    half = GDN_HEADS * GDN_DV
    mix = _dot(oa_ref[...], wo_ref[0:half, :]) + _dot(ob_ref[...], wo_ref[half:, :])
    out_ref[...] = _resid_mlp(x_ref[...], mix, gt1[...], sh2[...], sc2[...], gt2[...], g2[...], w1_ref, w2_ref)


def _mix1_mlp_body(x_ref, y_ref, gate_ref, gt1, sh2, sc2, gt2, g2, wo_ref, w1_ref, w2_ref, out_ref):
    mix = _mm(y_ref[...] * gate_ref[...], wo_ref[...])
    out_ref[...] = _resid_mlp(x_ref[...], mix, gt1[...], sh2[...], sc2[...], gt2[...], g2[...], w1_ref, w2_ref)


def _mix_mlp(body, name, x, mix_in, mods, g2, wo, w1, w2, tm):
    b, l, d = x.shape
    row = lambda width: pl.BlockSpec((None, tm, width), lambda bi, i: (bi, i, 0))
    mod = pl.BlockSpec((None, 1, d), lambda bi, i: (bi, 0, 0))
    return pl.pallas_call(
        body,
        grid=(b, l // tm),
        in_specs=[row(d)] + [row(a.shape[-1]) for a in mix_in] + [mod] * 4
                 + [_const_spec((1, d)), _const_spec(wo.shape), _const_spec(w1.shape), _const_spec(w2.shape)],
        out_specs=row(d),
        out_shape=jax.ShapeDtypeStruct((b, l, d), F32),
        compiler_params=_cparams(("arbitrary", "arbitrary")),
        name=name,
    )(x, *mix_in, *mods, g2, wo, w1, w2)


def _rwkv_proj_body(x_ref, shp_ref, sh_ref, sc_ref, g_ref, mu_ref, vec_ref,
                    wr_ref, wk_ref, wv_ref, w1_ref, w2_ref, a1_ref, a2_ref, g1_ref, g2_ref,
                    r_ref, lw_ref, k_ref, v_ref, a_ref, gate_ref, shift_ref, hext):
    tm = x_ref.shape[0]
    i = pl.program_id(1)
    h = _rms_mod(x_ref[...], g_ref[...], sc_ref[...], sh_ref[...])

    @pl.when(i == 0)
    def _():
        hext[SUBLANES - 1:SUBLANES, :] = shp_ref[...]

    hext[SUBLANES:SUBLANES + tm, :] = h
    h_prev = hext[SUBLANES - 1:SUBLANES - 1 + tm, :]
    hext[SUBLANES - 1:SUBLANES, :] = h[tm - 1:tm, :]
    shift_ref[...] = h[tm - 1:tm, :]

    xx = h_prev - h
    mu = mu_ref[...]
    vec = vec_ref[...]
    w0, a0 = vec[0:1], vec[1:2]
    mixed = lambda j: (h + xx * mu[j:j + 1]).astype(BF16)
    r_ref[...] = _dot(mixed(0), wr_ref[...])
    wl = w0 + _mm(jnp.tanh(_dot(mixed(1), w1_ref[...])), w2_ref[...])
    lw_ref[...] = -jnp.exp(-_softplus(-wl) - 0.5)
    k_ref[...] = _dot(mixed(2), wk_ref[...])
    v_ref[...] = _dot(mixed(3), wv_ref[...])
    a_ref[...] = _sigmoid(a0 + _mm(_dot(mixed(4), a1_ref[...]), a2_ref[...]))
    gate_ref[...] = _mm(_sigmoid(_dot(mixed(5), g1_ref[...])), g2_ref[...])


def _rwkv_proj(x, shift_prev, sh, sc, g, mu8, vec8, weights, tm):
    b, l, d = x.shape
    row = pl.BlockSpec((None, tm, d), lambda bi, i: (bi, i, 0))
    mod = pl.BlockSpec((None, 1, d), lambda bi, i: (bi, 0, 0))
    big = jax.ShapeDtypeStruct((b, l, d), F32)
    return pl.pallas_call(
        _rwkv_proj_body,
        grid=(b, l // tm),
        in_specs=[row, mod, mod, mod, _const_spec((1, d)), _const_spec((SUBLANES, d)), _const_spec((SUBLANES, d))]
                 + [_const_spec(w.shape) for w in weights],
        out_specs=[row] * 6 + [mod],
        out_shape=[big] * 6 + [jax.ShapeDtypeStruct((b, 1, d), F32)],
        scratch_shapes=[pltpu.VMEM((tm + SUBLANES, d), F32)],
        compiler_params=_cparams(("arbitrary", "arbitrary")),
        name="rwkv_proj",
    )(x, shift_prev, sh, sc, g, mu8, vec8, *weights)


def _wkv_body(r_ref, lw_ref, k_ref, v_ref, a_ref, vec_ref, s0_ref, y_ref, sout_ref, state, *, chunk):
    c = chunk
    t = pl.program_id(2)
    n = RWKV_HEAD

    @pl.when(t == 0)
    def _():
        state[...] = s0_ref[...]

    vec = vec_ref[...]
    k_k, k_a, r_k, ln_w, ln_b = vec[0:1], vec[1:2], vec[2:3], vec[3:4], vec[4:5]
    tri, strict, _ = _tri_masks(c)
    tri_b = tri.astype(BF16)
    lane = lax.broadcasted_iota(jnp.int32, (c, LANES), 1)
    first = lane < n

    for c0 in range(0, r_ref.shape[0], c):
        rows = slice(c0, c0 + c)
        r, lw, k, v, a = r_ref[rows, :], lw_ref[rows, :], k_ref[rows, :], v_ref[rows, :], a_ref[rows, :]
        kq = k * k_k
        sq = kq * kq
        ss = jnp.where(first,
                       jnp.sum(jnp.where(first, sq, 0.0), axis=-1, keepdims=True),
                       jnp.sum(jnp.where(first, 0.0, sq), axis=-1, keepdims=True))
        kkn = kq * lax.rsqrt(ss + NORM_EPS)
        kp = k * (1.0 + (a - 1.0) * k_a)
        cl = _mask_mm(tri_b, lw)
        e_neg = jnp.exp(-cl)
        at = -kkn * jnp.exp(cl - lw)
        rt = r * jnp.exp(cl)
        bt = (kkn * a) * e_neg
        kt = kp * e_neg
        gam = jnp.exp(cl[c - 1:c, :])
        rk = r * kp * r_k
        outs = []
        for e in range(LANES // n):
            ls = slice(e * n, (e + 1) * n)
            qs = jnp.concatenate([at[:, ls], rt[:, ls]], axis=0)
            ks = jnp.concatenate([bt[:, ls], kt[:, ls]], axis=0)
            vv = v[:, ls]
            s_e = state[e]
            m = _mm3(qs, ks, NT)
            qst = _mm3(qs, s_e, NT)
            a_ab = jnp.where(strict, m[:c, :c], 0.0)
            a_ak = jnp.where(strict, m[:c, c:], 0.0)
            a_rb = jnp.where(tri, m[c:, :c], 0.0)
            a_rk = jnp.where(tri, m[c:, c:], 0.0)
            u = _unit_lower_solve(a_ab, qst[:c] + _mm3(a_ak, vv), _mm3)
            y = qst[c:] + _mm3(a_rb, u) + _mm3(a_rk, vv)
            state[e] = (s_e + _mm3(u, bt[:, ls], TN) + _mm3(vv, kt[:, ls], TN)) * gam[:, ls]
            mean = jnp.mean(y, axis=-1, keepdims=True)
            yc = y - mean
            var = jnp.mean(yc * yc, axis=-1, keepdims=True)
            yn = yc * lax.rsqrt(var + RWKV_GN_EPS) * ln_w[:, ls] + ln_b[:, ls]
            bonus = jnp.sum(rk[:, ls], axis=-1, keepdims=True) * vv
            outs.append(yn + bonus)
        y_ref[rows, :] = jnp.concatenate(outs, axis=1)

    @pl.when(t == pl.num_programs(2) - 1)
    def _():
        sout_ref[...] = state[...]


def _wkv(r, lw, k, v, a, vec8, s0, chunk, tb):
    b, l, d = r.shape
    hp = LANES // RWKV_HEAD
    row = pl.BlockSpec((None, tb, LANES), lambda bi, p, t: (bi, t, p))
    st = pl.BlockSpec((None, hp, RWKV_HEAD, RWKV_HEAD), lambda bi, p, t: (bi, p, 0, 0))
    return pl.pallas_call(
        functools.partial(_wkv_body, chunk=chunk),
        grid=(b, d // LANES, l // tb),
        in_specs=[row] * 5 + [pl.BlockSpec((SUBLANES, LANES), lambda bi, p, t: (0, p)), st],
        out_specs=[row, st],
        out_shape=[jax.ShapeDtypeStruct((b, l, d), F32),
                   jax.ShapeDtypeStruct((b, RWKV_HEADS, RWKV_HEAD, RWKV_HEAD), F32)],
        scratch_shapes=[pltpu.VMEM((hp, RWKV_HEAD, RWKV_HEAD), F32)],
        compiler_params=_cparams(("arbitrary", "arbitrary", "arbitrary")),
        name="wkv7_chunked",
    )(r, lw, k, v, a, vec8, s0)


def _pad_rows(a, rows):
    return jnp.pad(a, ((0, rows - a.shape[0]), (0, 0)))


def _mods(m_rows):
    return [t[:, None, :] for t in jnp.split(m_rows, 6, axis=-1)]


def _layer0_weights(w_in, conv_w, a_log, dt_bias, fox_fbias, w_out, ff_w1, ff_w2):
    c = (0, 1536, 2048, 2052, 2056, 2568, 3080, 3592, 3596)
    seg = lambda i: w_in[:, c[i]:c[i + 1]]
    small = jnp.concatenate([seg(2), seg(3), seg(7)], axis=1)
    small = jnp.pad(small, ((0, 0), (0, LANES - small.shape[1])))
    w_all = jnp.concatenate([seg(0), seg(1), seg(4), seg(5), seg(6), small], axis=1).astype(BF16)
    aux = jnp.zeros((SUBLANES, LANES), F32)
    aux = aux.at[0, 4:8].set(a_log).at[1, 4:8].set(dt_bias).at[2, 8:12].set(fox_fbias)
    return dict(w_all=w_all, aux=aux, conv_w8=_pad_rows(conv_w, SUBLANES),
                w_out=w_out.astype(BF16), ff_w1=ff_w1.astype(BF16), ff_w2=ff_w2.astype(BF16))


def _hybrid_layer(x, mods, conv_buf, s0, fox_cache, wts, norm_mix, norm_ff, gdn_onorm, fox_qnorm, fox_knorm,
                  tm, chunk, blk):
    b, l, d = x.shape
    sh1, sc1, gt1, sh2, sc2, gt2 = mods
    row = lambda g: g.reshape(1, -1)
    qkv_raw, z, small, q_b, k_f, v_f, k_b, v_b = _inproj0(
        x, sh1, sc1, row(norm_mix), wts["w_all"], row(fox_qnorm), row(fox_knorm), wts["aux"], tm)
    conv_new = qkv_raw[:, l - (GDN_CONV - 1):, :]
    logf = small[:, :, 8:12]
    conv_init8 = jnp.pad(conv_buf, ((0, 0), (SUBLANES - (GDN_CONV - 1), 0), (0, 0)))
    o_a, s_new = _gdn(qkv_raw, z, small, conv_init8, s0, wts["conv_w8"], row(gdn_onorm), chunk)

    if fox_cache is None:
        lf = jnp.transpose(logf[0]).reshape(FOX_HEADS, l // LANES, LANES)
        c_rows = _cumsum_rows(lf).reshape(FOX_HEADS, l // blk, 1, blk)
        o_b = _fox_prompt(q_b[0], k_b[0], v_b[0], c_rows, blk)[None]
    else:
        k_cache, v_cache, logf_cache = fox_cache
        p = k_cache.shape[1]
        total = p + l
        padded = -(-total // (SUBLANES * LANES)) * (SUBLANES * LANES)
        lf = jnp.concatenate([jnp.transpose(logf_cache, (0, 2, 1)), jnp.transpose(logf, (0, 2, 1))], axis=2)
        lf = jnp.pad(lf, ((0, 0), (0, 0), (padded - total, 0)))
        cs = _cumsum_rows(lf.reshape(b * FOX_HEADS, padded // LANES, LANES)).reshape(b, FOX_HEADS, padded)
        c_cache = cs[:, :, None, padded - total:padded - l]
        c_new = cs[:, :, padded - l:]
        o_b = _fox_step(q_b, k_f, v_f, k_cache.reshape(b, p, FOX_W), v_cache.reshape(b, p, FOX_W),
                        c_cache, c_new[:, :, None, :], c_new[:, :, :, None])

    x = _mix_mlp(_mix0_mlp_body, "l0_out_mlp", x, [o_a, o_b], [gt1, sh2, sc2, gt2], row(norm_ff),
                 wts["w_out"], wts["ff_w1"], wts["ff_w2"], tm)
    return (x, conv_new, s_new, k_f.reshape(b, l, FOX_HEADS, FOX_DH), v_f.reshape(b, l, FOX_HEADS, FOX_DH), logf)


def _layer1_weights(mu, w_r, w_k, w_v, w0, w1, w2, a0, a1, a2, g1, g2, k_k, k_a, r_k, ln_w, ln_b, w_o, ff_w1, ff_w2):
    padc = lambda w: jnp.pad(w, ((0, 0), (0, -w.shape[1] % LANES))).astype(BF16)
    padr = lambda w: jnp.pad(w, ((0, -w.shape[0] % LANES), (0, 0))).astype(BF16)
    proj = [w_r.astype(BF16), w_k.astype(BF16), w_v.astype(BF16),
            padc(w1), padr(w2), padc(a1), padr(a2), padc(g1), padr(g2)]
    return dict(mu8=_pad_rows(mu, SUBLANES), pvec=_pad_rows(jnp.stack([w0, a0]), SUBLANES),
                svec=_pad_rows(jnp.stack([k_k, k_a, r_k.reshape(-1), ln_w, ln_b]), SUBLANES),
                proj=proj, w_o=w_o.astype(BF16), ff_w1=ff_w1.astype(BF16), ff_w2=ff_w2.astype(BF16))


def _rwkv_layer(x, mods, shift_prev, s0, wts, norm_mix, norm_ff, tm, chunk, tb):
    sh1, sc1, gt1, sh2, sc2, gt2 = mods
    row = lambda g: g.reshape(1, -1)
    r, lw, k, v, a, gate, shift_new = _rwkv_proj(
        x, shift_prev[:, None, :], sh1, sc1, row(norm_mix), wts["mu8"], wts["pvec"], wts["proj"], tm)
    y, s_new = _wkv(r, lw, k, v, a, wts["svec"], s0, chunk, tb)
    x = _mix_mlp(_mix1_mlp_body, "l1_out_mlp", x, [y, gate], [gt1, sh2, sc2, gt2], row(norm_ff),
                 wts["w_o"], wts["ff_w1"], wts["ff_w2"], tm)
    return x, shift_new[:, 0, :], s_new


def kernel(x_prompt, x_sample, c_prompt, c_sample, cache_l0_conv, state_l0_delta, cache_l0_fox_k, cache_l0_fox_v, cache_l0_fox_logf, state_l1_shift, state_l1_wkv, l0_ada_w, l0_ada_b, l0_norm_mix, l0_norm_ff, l0_w_in, l0_conv_w, l0_a_log, l0_dt_bias, l0_gdn_onorm, l0_fox_qnorm, l0_fox_knorm, l0_fox_fbias, l0_w_out, l0_ff_w1, l0_ff_w2, l1_ada_w, l1_ada_b, l1_norm_mix, l1_norm_ff, l1_mu, l1_w_r, l1_w_k, l1_w_v, l1_w0, l1_w1, l1_w2, l1_a0, l1_a1, l1_a2, l1_g1, l1_g2, l1_k_k, l1_k_a, l1_r_k, l1_ln_w, l1_ln_b, l1_w_o, l1_ff_w1, l1_ff_w2):
    bp, lp, d = x_prompt.shape
    bs, ls, _ = x_sample.shape
    c_all = _pad_rows(jnp.concatenate([c_prompt, c_sample], axis=0), 2 * SUBLANES)
    m0 = _ada(c_all, l0_ada_w, l0_ada_b)
    m1 = _ada(c_all, l1_ada_w, l1_ada_b)
    mods0_p, mods0_s = _mods(m0[:bp]), _mods(m0[bp:bp + bs])
    mods1_p, mods1_s = _mods(m1[:bp]), _mods(m1[bp:bp + bs])

    w0 = _layer0_weights(l0_w_in, l0_conv_w, l0_a_log, l0_dt_bias, l0_fox_fbias, l0_w_out, l0_ff_w1, l0_ff_w2)
    w1 = _layer1_weights(l1_mu, l1_w_r, l1_w_k, l1_w_v, l1_w0, l1_w1, l1_w2, l1_a0, l1_a1, l1_a2, l1_g1, l1_g2,
                         l1_k_k, l1_k_a, l1_r_k, l1_ln_w, l1_ln_b, l1_w_o, l1_ff_w1, l1_ff_w2)
    norms0 = (l0_norm_mix, l0_norm_ff, l0_gdn_onorm, l0_fox_qnorm, l0_fox_knorm)

    tm_p = min(512, lp)
    y_p, conv_p, delta_p, fox_k_p, fox_v_p, fox_logf_p = _hybrid_layer(
        x_prompt, mods0_p, jnp.zeros((bp, GDN_CONV - 1, GDN_QKV), F32),
        jnp.zeros((bp, GDN_HEADS, GDN_DK, GDN_DV), F32), None, w0, *norms0,
        tm=tm_p, chunk=min(GDN_CHUNK, lp), blk=min(512, lp))
    y_s, conv_s, delta_s, fox_k_s, fox_v_s, fox_logf_s = _hybrid_layer(
        x_sample, mods0_s, cache_l0_conv, state_l0_delta,
        (cache_l0_fox_k, cache_l0_fox_v, cache_l0_fox_logf), w0, *norms0,
        tm=ls, chunk=min(GDN_CHUNK, ls), blk=None)

    y_p, shift_p, wkv_p = _rwkv_layer(
        y_p, mods1_p, jnp.zeros((bp, d), F32), jnp.zeros((bp, RWKV_HEADS, RWKV_HEAD, RWKV_HEAD), F32),
        w1, l1_norm_mix, l1_norm_ff, tm=tm_p, chunk=min(RWKV_CHUNK, lp), tb=min(256, lp))
    y_s, shift_s, wkv_s = _rwkv_layer(
        y_s, mods1_s, state_l1_shift, state_l1_wkv, w1, l1_norm_mix, l1_norm_ff,
        tm=ls, chunk=min(RWKV_CHUNK, ls), tb=ls)

    return (y_p, y_s, conv_p, conv_s, delta_p, delta_s, fox_k_p, fox_k_s,
            fox_v_p, fox_v_s, fox_logf_p, fox_logf_s, shift_p, shift_s, wkv_p, wkv_s)
```

```python
import functools
import math

import jax
import jax.numpy as jnp
from jax import lax
from jax.experimental import pallas as pl
from jax.experimental.pallas import tpu as pltpu

F32 = jnp.float32
BF16 = jnp.bfloat16

D_MODEL = 1024
D_FF = 4 * D_MODEL
NORM_EPS = 1e-6

GDN_HEADS = 4
GDN_DK = 128
GDN_DV = 128
GDN_CONV = 4
GDN_QKV = GDN_HEADS * (2 * GDN_DK + GDN_DV)
GDN_CHUNK = 64

FOX_HEADS = 4
FOX_DH = 128
FOX_W = FOX_HEADS * FOX_DH

RWKV_HEAD = 64
RWKV_HEADS = D_MODEL // RWKV_HEAD
RWKV_GN_EPS = 1e-5 * RWKV_HEAD
RWKV_CHUNK = 64

LANES = 128
SUBLANES = 8
VMEM_LIMIT = 56 * 1024 * 1024

NN = (((1,), (0,)), ((), ()))
NT = (((1,), (1,)), ((), ()))
TN = (((0,), (0,)), ((), ()))


def _dot(a, b, dims=NN):
    return lax.dot_general(a, b, dims, preferred_element_type=F32)


def _mm(a, b, dims=NN):
    return _dot(a.astype(BF16), b.astype(BF16), dims)


def _split2(x):
    hi = x.astype(BF16)
    lo = (x - hi.astype(F32)).astype(BF16)
    return hi, lo


def _split3(x):
    hi = x.astype(BF16)
    r = x - hi.astype(F32)
    mid = r.astype(BF16)
    lo = (r - mid.astype(F32)).astype(BF16)
    return hi, mid, lo


def _mm3(a, b, dims=NN):
    ah, al = _split2(a)
    bh, bl = _split2(b)
    return _dot(ah, bh, dims) + (_dot(ah, bl, dims) + _dot(al, bh, dims))


def _mask_mm(mask01, x):
    x1, x2, x3 = _split3(x)
    return _dot(mask01, x1) + (_dot(mask01, x2) + _dot(mask01, x3))


def _sigmoid(x):
    return 1.0 / (1.0 + jnp.exp(-x))


def _softplus(x):
    return jnp.maximum(x, 0.0) + jnp.log(1.0 + jnp.exp(-jnp.abs(x)))


def _silu(x):
    return x * _sigmoid(x)


def _rms_mod(x, g, sc, sh):
    ms = jnp.mean(x * x, axis=-1, keepdims=True)
    return (x * lax.rsqrt(ms + NORM_EPS) * g) * (1.0 + sc) + sh


def _unit_lower_solve(n, x, mm):
    size = n.shape[0]
    levels = max(1, int(math.ceil(math.log2(size))))
    for lvl in range(levels):
        x = x + mm(n, x)
        if lvl + 1 < levels:
            n = mm(n, n)
    return x


def _tri_masks(c):
    row = lax.broadcasted_iota(jnp.int32, (c, c), 0)
    col = lax.broadcasted_iota(jnp.int32, (c, c), 1)
    return row >= col, row > col, row == col


def _cparams(sem):
    return pltpu.CompilerParams(dimension_semantics=sem, vmem_limit_bytes=VMEM_LIMIT)


def _const_spec(shape):
    nd = len(shape)
    return pl.BlockSpec(shape, lambda *_: (0,) * nd, pipeline_mode=pl.Buffered(1))


def _ada_body(c_ref, w_ref, b_ref, o_ref):
    o_ref[...] = _mm3(_silu(c_ref[...]), w_ref[...]) + b_ref[...]


def _ada(c_all, w, b):
    rows, d = c_all.shape
    n = w.shape[1]
    tn = 512
    return pl.pallas_call(
        _ada_body,
        grid=(n // tn,),
        in_specs=[pl.BlockSpec((rows, d), lambda j: (0, 0)),
                  pl.BlockSpec((d, tn), lambda j: (0, j)),
                  pl.BlockSpec((1, tn), lambda j: (0, j))],
        out_specs=pl.BlockSpec((rows, tn), lambda j: (0, j)),
        out_shape=jax.ShapeDtypeStruct((rows, n), F32),
        compiler_params=_cparams(("arbitrary",)),
        name="ada_ln",
    )(c_all, w, b.reshape(1, n))


_IN0_QKV = (0, GDN_QKV)
_IN0_Z = (GDN_QKV, GDN_QKV + 512)
_IN0_Q = (2048, 2560)
_IN0_K = (2560, 3072)
_IN0_V = (3072, 3584)
_IN0_S = (3584, 3712)
_IN0_COLS = 3712


def _inproj0_body(x_ref, sh_ref, sc_ref, g_ref, w_ref, qn_ref, kn_ref, aux_ref,
                  qkv_ref, z_ref, small_ref, qb_ref, kf_ref, vf_ref, kb_ref, vb_ref):
    h = _rms_mod(x_ref[...], g_ref[...], sc_ref[...], sh_ref[...]).astype(BF16)
    qkv_ref[...] = _dot(h, w_ref[:, _IN0_QKV[0]:_IN0_QKV[1]])
    z_ref[...] = _dot(h, w_ref[:, _IN0_Z[0]:_IN0_Z[1]])
    q = _dot(h, w_ref[:, _IN0_Q[0]:_IN0_Q[1]])
    k = _dot(h, w_ref[:, _IN0_K[0]:_IN0_K[1]])
    v = _dot(h, w_ref[:, _IN0_V[0]:_IN0_V[1]])
    raw = _dot(h, w_ref[:, _IN0_S[0]:_IN0_S[1]])
    qn = qn_ref[...]
    kn = kn_ref[...]
    for hd in range(FOX_HEADS):
        sl = slice(hd * FOX_DH, (hd + 1) * FOX_DH)
        qh = q[:, sl]
        kh = k[:, sl]
        qh = qh * lax.rsqrt(jnp.mean(qh * qh, axis=-1, keepdims=True) + NORM_EPS) * qn * FOX_DH ** -0.5
        kh = kh * lax.rsqrt(jnp.mean(kh * kh, axis=-1, keepdims=True) + NORM_EPS) * kn
        qb_ref[:, sl] = qh.astype(BF16)
        kf_ref[:, sl] = kh
        kb_ref[:, sl] = kh.astype(BF16)
    vf_ref[...] = v
    vb_ref[...] = v.astype(BF16)
    aux = aux_ref[...]
    a_log, dt_bias, f_bias = aux[0:1], aux[1:2], aux[2:3]
    lane = lax.broadcasted_iota(jnp.int32, raw.shape, 1)
    beta = _sigmoid(raw)
    gdec = -jnp.exp(a_log) * _softplus(raw + dt_bias)
    logf = -_softplus(-(raw + f_bias))
    small_ref[...] = jnp.where(lane < 4, beta, jnp.where(lane < 8, gdec, jnp.where(lane < 12, logf, 0.0)))


def _inproj0(x, sh, sc, g, w_all, qn, kn, aux, tm):
    b, l, d = x.shape
    row = lambda width: pl.BlockSpec((None, tm, width), lambda bi, i: (bi, i, 0))
    mod = pl.BlockSpec((None, 1, d), lambda bi, i: (bi, 0, 0))
    out = lambda width, dt: jax.ShapeDtypeStruct((b, l, width), dt)
    return pl.pallas_call(
        _inproj0_body,
        grid=(b, l // tm),
        in_specs=[row(d), mod, mod, _const_spec((1, d)), _const_spec((d, _IN0_COLS)),
                  _const_spec((1, FOX_DH)), _const_spec((1, FOX_DH)), _const_spec((SUBLANES, LANES))],
        out_specs=[row(GDN_QKV), row(512), row(LANES), row(FOX_W), row(FOX_W), row(FOX_W), row(FOX_W), row(FOX_W)],
        out_shape=[out(GDN_QKV, F32), out(512, F32), out(LANES, F32), out(FOX_W, BF16),
                   out(FOX_W, F32), out(FOX_W, F32), out(FOX_W, BF16), out(FOX_W, BF16)],
        compiler_params=_cparams(("arbitrary", "arbitrary")),
        name="l0_in_proj",
    )(x, sh, sc, g, w_all, qn, kn, aux)


def _chunk_masks(tb, chunk):
    shift = int(math.log2(chunk))
    row = lax.broadcasted_iota(jnp.int32, (tb, tb), 0)
    col = lax.broadcasted_iota(jnp.int32, (tb, tb), 1)
    same = (row >> shift) == (col >> shift)
    return same & (row >= col), same & (row > col), row == col


def _gdn_body(qkv_ref, z_ref, small_ref, cinit_ref, s0_ref, cw_ref, on_ref,
              o_ref, sout_ref, ext, state, *, chunk):
    c = chunk
    tb = qkv_ref.shape[0]
    ci = pl.program_id(1)

    @pl.when(ci == 0)
    def _():
        ext[0:SUBLANES, :] = cinit_ref[...]
        state[...] = s0_ref[...]

    ext[SUBLANES:SUBLANES + tb, :] = qkv_ref[...]
    cw = cw_ref[...]
    base = SUBLANES - (GDN_CONV - 1)
    y = ext[base:base + tb, :] * cw[0:1]
    for i in range(1, GDN_CONV):
        y = y + ext[base + i:base + i + tb, :] * cw[i:i + 1]
    qkv = _silu(y)
    ext[0:SUBLANES, :] = ext[tb:tb + SUBLANES, :]

    tri, strict, eye = _chunk_masks(tb, c)
    sm = small_ref[...]
    gc_all = _mask_mm(tri.astype(BF16), sm)
    onorm = on_ref[...]
    eye_d = _tri_masks(GDN_DK)[2]
    for hd in range(GDN_HEADS):
        q = qkv[:, hd * GDN_DK:(hd + 1) * GDN_DK]
        k = qkv[:, 512 + hd * GDN_DK:512 + (hd + 1) * GDN_DK]
        v = qkv[:, 1024 + hd * GDN_DV:1024 + (hd + 1) * GDN_DV]
        q = q * lax.rsqrt(jnp.sum(q * q, axis=-1, keepdims=True) + NORM_EPS) * GDN_DK ** -0.5
        k = k * lax.rsqrt(jnp.sum(k * k, axis=-1, keepdims=True) + NORM_EPS)
        beta = sm[:, hd:hd + 1]
        gc = gc_all[:, 4 + hd:5 + hd]
        gc_row = jnp.sum(jnp.where(eye, gc, 0.0), axis=0, keepdims=True)
        dec = jnp.where(tri, jnp.exp(jnp.where(tri, gc - gc_row, 0.0)), 0.0)
        kb = k.astype(BF16)
        a_strict = jnp.where(strict, beta * _dot(kb, kb, NT) * dec, 0.0)
        egc = jnp.exp(gc)
        rhs = jnp.concatenate([k * (beta * egc), v * beta], axis=1)
        sol = _unit_lower_solve(-a_strict, rhs, _mm3)
        qk = _dot(q.astype(BF16), kb, NT) * dec
        qs = _mm(qk, sol)
        oq = q * egc - qs[:, :GDN_DK]
        o0 = qs[:, GDN_DK:]
        s_h = state[hd]
        for c0 in range(0, tb, c):
            rows = slice(c0, c0 + c)
            g_last = gc_all[c0 + c - 1:c0 + c, 4 + hd:5 + hd]
            ks = _mm3(k[rows] * jnp.exp(g_last - gc_all[rows, 4 + hd:5 + hd]), sol[rows], TN)
            o = _mm3(oq[rows], s_h) + o0[rows]
            s_h = _mm3(jnp.where(eye_d, jnp.exp(g_last), 0.0) - ks[:, :GDN_DK], s_h) + ks[:, GDN_DK:]
            zz = z_ref[rows, hd * GDN_DV:(hd + 1) * GDN_DV]
            o = o * lax.rsqrt(jnp.mean(o * o, axis=-1, keepdims=True) + NORM_EPS) * onorm
            o_ref[rows, hd * GDN_DV:(hd + 1) * GDN_DV] = (o * _silu(zz)).astype(BF16)
        state[hd] = s_h

    @pl.when(ci == pl.num_programs(1) - 1)
    def _():
        sout_ref[...] = state[...]


def _gdn(qkv_raw, z, small, conv_init8, s0, conv_w8, onorm, chunk, tb):
    b, l, _ = qkv_raw.shape
    row = lambda width: pl.BlockSpec((None, tb, width), lambda bi, i: (bi, i, 0))
    st = pl.BlockSpec((None, GDN_HEADS, GDN_DK, GDN_DV), lambda bi, i: (bi, 0, 0, 0))
    return pl.pallas_call(
        functools.partial(_gdn_body, chunk=chunk),
        grid=(b, l // tb),
        in_specs=[row(GDN_QKV), row(512), row(LANES),
                  pl.BlockSpec((None, SUBLANES, GDN_QKV), lambda bi, i: (bi, 0, 0)), st,
                  _const_spec((SUBLANES, GDN_QKV)), _const_spec((1, GDN_DV))],
        out_specs=[row(512), st],
        out_shape=[jax.ShapeDtypeStruct((b, l, 512), BF16),
                   jax.ShapeDtypeStruct((b, GDN_HEADS, GDN_DK, GDN_DV), F32)],
        scratch_shapes=[pltpu.VMEM((tb + SUBLANES, GDN_QKV), F32),
                        pltpu.VMEM((GDN_HEADS, GDN_DK, GDN_DV), F32)],
        compiler_params=_cparams(("arbitrary", "arbitrary")),
        name="gdn_mixer",
    )(qkv_raw, z, small, conv_init8, s0, conv_w8, onorm)


def _cumsum_body(x_ref, o_ref):
    x = x_ref[...]
    rows = x.shape[0]
    li = lax.broadcasted_iota(jnp.int32, (LANES, LANES), 0)
    lj = lax.broadcasted_iota(jnp.int32, (LANES, LANES), 1)
    upper = (li <= lj).astype(BF16)
    ones = jnp.ones((LANES, LANES), BF16)
    ri = lax.broadcasted_iota(jnp.int32, (rows, rows), 0)
    rj = lax.broadcasted_iota(jnp.int32, (rows, rows), 1)
    before = (rj < ri).astype(BF16)
    x1, x2, x3 = _split3(x)
    within = _dot(x1, upper) + (_dot(x2, upper) + _dot(x3, upper))
    total = _dot(x1, ones) + (_dot(x2, ones) + _dot(x3, ones))
    o_ref[...] = within + _mask_mm(before, total)


def _cumsum_rows(x):
    g, rows, _ = x.shape
    spec = pl.BlockSpec((None, rows, LANES), lambda i: (i, 0, 0))
    return pl.pallas_call(
        _cumsum_body, grid=(g,), in_specs=[spec], out_specs=spec,
        out_shape=jax.ShapeDtypeStruct(x.shape, F32),
        compiler_params=_cparams(("arbitrary",)),
        name="logf_cumsum",
    )(x)


def _fox_prompt_body(q_ref, k_ref, v_ref, c_ref, o_ref, *, blk):
    qi = pl.program_id(1)
    q = q_ref[...]
    c0 = c_ref[qi][:, 0:1]

    def scores(j):
        kj = k_ref[pl.ds(pl.multiple_of(j * blk, blk), blk), :]
        return _dot(q, kj, NT) + (c0 - c_ref[j])

    def update(j, s, carry):
        m, l, acc = carry
        vj = v_ref[pl.ds(pl.multiple_of(j * blk, blk), blk), :]
        m_new = jnp.maximum(m, jnp.max(s, axis=-1, keepdims=True))
        alpha = jnp.exp(m - m_new)
        p = jnp.exp(s - m_new)
        l = alpha * l + jnp.sum(p, axis=-1, keepdims=True)
        acc = alpha * acc + _dot(p.astype(BF16), vj)
        return m_new, l, acc

    init = (jnp.full((blk, 1), -jnp.inf, F32), jnp.zeros((blk, 1), F32), jnp.zeros((blk, FOX_DH), F32))
    carry = lax.fori_loop(0, qi, lambda j, cr: update(j, scores(j), cr), init)
    tri, _, _ = _tri_masks(blk)
    m, l, acc = update(qi, jnp.where(tri, scores(qi), -jnp.inf), carry)
    o_ref[...] = (acc / l).astype(BF16)


def _fox_prompt(q, k, v, c_rows, blk):
    l = q.shape[0]
    nb = l // blk
    return pl.pallas_call(
        functools.partial(_fox_prompt_body, blk=blk),
        grid=(FOX_HEADS, nb),
        in_specs=[pl.BlockSpec((blk, FOX_DH), lambda h, i: (i, h)),
                  pl.BlockSpec((l, FOX_DH), lambda h, i: (0, h)),
                  pl.BlockSpec((l, FOX_DH), lambda h, i: (0, h)),
                  pl.BlockSpec((None, nb, 1, blk), lambda h, i: (h, 0, 0, 0))],
        out_specs=pl.BlockSpec((blk, FOX_DH), lambda h, i: (i, h)),
        out_shape=jax.ShapeDtypeStruct((l, FOX_W), BF16),
        compiler_params=_cparams(("arbitrary", "arbitrary")),
        name="fox_prompt_attn",
    )(q, k, v, c_rows)


def _fox_step_body(q_ref, kn_ref, vn_ref, kc_ref, vc_ref, cc_ref, cnr_ref, cnc_ref, o_ref):
    q = q_ref[...]
    n = q.shape[0]
    c_i = cnc_ref[...]
    s_c = _mm(q, kc_ref[...], NT) + (c_i - cc_ref[...])
    s_n = _mm(q, kn_ref[...], NT) + (c_i - cnr_ref[...])
    tri, _, _ = _tri_masks(n)
    s_n = jnp.where(tri, s_n, -jnp.inf)
    m = jnp.maximum(jnp.max(s_c, axis=-1, keepdims=True), jnp.max(s_n, axis=-1, keepdims=True))
    p_c = jnp.exp(s_c - m)
    p_n = jnp.exp(s_n - m)
    l = jnp.sum(p_c, axis=-1, keepdims=True) + jnp.sum(p_n, axis=-1, keepdims=True)
    o = _mm(p_c, vc_ref[...]) + _mm(p_n, vn_ref[...])
    o_ref[...] = (o / l).astype(BF16)


def _fox_step(q, k_new, v_new, k_cache, v_cache, c_cache, c_new_row, c_new_col):
    b, n, _ = q.shape
    p = k_cache.shape[1]
    new = pl.BlockSpec((None, n, FOX_DH), lambda bi, h: (bi, 0, h))
    cache = pl.BlockSpec((None, p, FOX_DH), lambda bi, h: (bi, 0, h))
    return pl.pallas_call(
        _fox_step_body,
        grid=(b, FOX_HEADS),
        in_specs=[new, new, new, cache, cache,
                  pl.BlockSpec((None, None, 1, p), lambda bi, h: (bi, h, 0, 0)),
                  pl.BlockSpec((None, None, 1, n), lambda bi, h: (bi, h, 0, 0)),
                  pl.BlockSpec((None, None, n, 1), lambda bi, h: (bi, h, 0, 0))],
        out_specs=new,
        out_shape=jax.ShapeDtypeStruct((b, n, FOX_W), BF16),
        compiler_params=_cparams(("arbitrary", "arbitrary")),
        name="fox_step_attn",
    )(q, k_new, v_new, k_cache, v_cache, c_cache, c_new_row, c_new_col)


def _resid_mlp(x, mix, gt1, sh2, sc2, gt2, g2, w1_ref, w2_ref):
    x1 = x + gt1 * mix
    h2 = _rms_mod(x1, g2, sc2, sh2).astype(BF16)
    acc = jnp.zeros_like(x1)
    step = D_MODEL
    for c0 in range(0, D_FF, step):
        hid = jnp.maximum(_dot(h2, w1_ref[:, c0:c0 + step]), 0.0)
        acc = acc + _dot((hid * hid).astype(BF16), w2_ref[c0:c0 + step, :])
    return x1 + gt2 * acc


def _mix0_mlp_body(x_ref, oa_ref, ob_ref, gt1, sh2, sc2, gt2, g2, wo_ref, w1_ref, w2_ref, out_ref):
    half = GDN_HEADS * GDN_DV
    mix = _dot(oa_ref[...], wo_ref[0:half, :]) + _dot(ob_ref[...], wo_ref[half:, :])
    out_ref[...] = _resid_mlp(x_ref[...], mix, gt1[...], sh2[...], sc2[...], gt2[...], g2[...], w1_ref, w2_ref)


def _mix1_mlp_body(x_ref, y_ref, gate_ref, gt1, sh2, sc2, gt2, g2, wo_ref, w1_ref, w2_ref, out_ref):
    mix = _mm(y_ref[...] * gate_ref[...], wo_ref[...])
    out_ref[...] = _resid_mlp(x_ref[...], mix, gt1[...], sh2[...], sc2[...], gt2[...], g2[...], w1_ref, w2_ref)


def _mix_mlp(body, name, x, mix_in, mods, g2, wo, w1, w2, tm):
    b, l, d = x.shape
    row = lambda width: pl.BlockSpec((None, tm, width), lambda bi, i: (bi, i, 0))
    mod = pl.BlockSpec((None, 1, d), lambda bi, i: (bi, 0, 0))
    return pl.pallas_call(
        body,
        grid=(b, l // tm),
        in_specs=[row(d)] + [row(a.shape[-1]) for a in mix_in] + [mod] * 4
                 + [_const_spec((1, d)), _const_spec(wo.shape), _const_spec(w1.shape), _const_spec(w2.shape)],
        out_specs=row(d),
        out_shape=jax.ShapeDtypeStruct((b, l, d), F32),
        compiler_params=_cparams(("arbitrary", "arbitrary")),
        name=name,
    )(x, *mix_in, *mods, g2, wo, w1, w2)


def _rwkv_proj_body(x_ref, shp_ref, sh_ref, sc_ref, g_ref, mu_ref, vec_ref,
                    wr_ref, wk_ref, wv_ref, w1_ref, w2_ref, a1_ref, a2_ref, g1_ref, g2_ref,
                    r_ref, lw_ref, k_ref, v_ref, a_ref, gate_ref, shift_ref, hext):
    tm = x_ref.shape[0]
    i = pl.program_id(1)
    h = _rms_mod(x_ref[...], g_ref[...], sc_ref[...], sh_ref[...])

    @pl.when(i == 0)
    def _():
        hext[SUBLANES - 1:SUBLANES, :] = shp_ref[...]

    hext[SUBLANES:SUBLANES + tm, :] = h
    h_prev = hext[SUBLANES - 1:SUBLANES - 1 + tm, :]
    hext[SUBLANES - 1:SUBLANES, :] = h[tm - 1:tm, :]
    shift_ref[...] = h[tm - 1:tm, :]

    xx = h_prev - h
    mu = mu_ref[...]
    vec = vec_ref[...]
    w0, a0 = vec[0:1], vec[1:2]
    mixed = lambda j: (h + xx * mu[j:j + 1]).astype(BF16)
    r_ref[...] = _dot(mixed(0), wr_ref[...])
    wl = w0 + _mm(jnp.tanh(_dot(mixed(1), w1_ref[...])), w2_ref[...])
    lw_ref[...] = -jnp.exp(-_softplus(-wl) - 0.5)
    k_ref[...] = _dot(mixed(2), wk_ref[...])
    v_ref[...] = _dot(mixed(3), wv_ref[...])
    a_ref[...] = _sigmoid(a0 + _mm(_dot(mixed(4), a1_ref[...]), a2_ref[...]))
    gate_ref[...] = _mm(_sigmoid(_dot(mixed(5), g1_ref[...])), g2_ref[...])


def _rwkv_proj(x, shift_prev, sh, sc, g, mu8, vec8, weights, tm):
    b, l, d = x.shape
    row = pl.BlockSpec((None, tm, d), lambda bi, i: (bi, i, 0))
    mod = pl.BlockSpec((None, 1, d), lambda bi, i: (bi, 0, 0))
    big = jax.ShapeDtypeStruct((b, l, d), F32)
    return pl.pallas_call(
        _rwkv_proj_body,
        grid=(b, l // tm),
        in_specs=[row, mod, mod, mod, _const_spec((1, d)), _const_spec((SUBLANES, d)), _const_spec((SUBLANES, d))]
                 + [_const_spec(w.shape) for w in weights],
        out_specs=[row] * 6 + [mod],
        out_shape=[big] * 6 + [jax.ShapeDtypeStruct((b, 1, d), F32)],
        scratch_shapes=[pltpu.VMEM((tm + SUBLANES, d), F32)],
        compiler_params=_cparams(("arbitrary", "arbitrary")),
        name="rwkv_proj",
    )(x, shift_prev, sh, sc, g, mu8, vec8, *weights)


def _pair_sum(x, first):
    lo = jnp.sum(jnp.where(first, x, 0.0), axis=-1, keepdims=True)
    hi = jnp.sum(jnp.where(first, 0.0, x), axis=-1, keepdims=True)
    return jnp.where(first, lo, hi)


def _wkv_body(r_ref, lw_ref, k_ref, v_ref, a_ref, vec_ref, s0_ref, y_ref, sout_ref, state, *, chunk):
    c = chunk
    tb = r_ref.shape[0]
    n = RWKV_HEAD
    t = pl.program_id(2)

    @pl.when(t == 0)
    def _():
        state[...] = jnp.zeros_like(state)
        state[0:n, 0:n] = s0_ref[0]
        state[n:, n:] = s0_ref[1]

    vec = vec_ref[...]
    k_k, k_a, r_k, ln_w, ln_b = vec[0:1], vec[1:2], vec[2:3], vec[3:4], vec[4:5]
    tri, strict, eye = _chunk_masks(tb, c)
    first = lax.broadcasted_iota(jnp.int32, (tb, LANES), 1) < n
    hr = lax.broadcasted_iota(jnp.int32, (LANES, LANES), 0)
    hc = lax.broadcasted_iota(jnp.int32, (LANES, LANES), 1)
    same_head = (hr < n) == (hc < n)
    eye_l = jnp.where(hr == hc, 1.0, 0.0)

    r, lw, k, v, a = r_ref[...], lw_ref[...], k_ref[...], v_ref[...], a_ref[...]
    kq = k * k_k
    kkn = kq * lax.rsqrt(_pair_sum(kq * kq, first) + NORM_EPS)
    kp = k * (1.0 + (a - 1.0) * k_a)
    cl = _mask_mm(tri.astype(BF16), lw)
    e_neg = jnp.exp(-cl)
    at = -kkn * jnp.exp(cl - lw)
    rt = r * jnp.exp(cl)
    bt = (kkn * a) * e_neg
    kt = kp * e_neg
    bonus = _pair_sum(r * kp * r_k, first) * v
    kcat = jnp.concatenate([bt, kt], axis=0).astype(BF16)
    vb = v.astype(BF16)
    levels = max(1, int(math.ceil(math.log2(c))))

    was, u0s, arbs, avrs = [], [], [], []
    for head_lanes in (first, jnp.logical_not(first)):
        at_e = jnp.where(head_lanes, at, 0.0)
        ma = _dot(at_e.astype(BF16), kcat, NT)
        mr = _dot(jnp.where(head_lanes, rt, 0.0).astype(BF16), kcat, NT)
        nk = jnp.where(strict, ma[:, :tb], 0.0)
        av_a = _dot(jnp.where(strict, ma[:, tb:], 0.0).astype(BF16), vb)
        avrs.append(_dot(jnp.where(tri, mr[:, tb:], 0.0).astype(BF16), vb))
        arbs.append(jnp.where(tri, mr[:, :tb], 0.0).astype(BF16))
        t_inv = jnp.where(eye, 1.0, nk)
        for _ in range(1, levels):
            nk = _mm(nk, nk)
            t_inv = t_inv + _mm(nk, t_inv)
        w = _mm(t_inv, jnp.concatenate([at_e, av_a], axis=1))
        was.append(w[:, :LANES])
        u0s.append(w[:, LANES:])
    wa = was[0] + was[1]
    u0 = jnp.where(first, u0s[0], u0s[1])
    wcat = jnp.concatenate([wa, u0], axis=1).astype(BF16)
    aw0 = _dot(arbs[0], wcat)
    aw1 = _dot(arbs[1], wcat)
    rq = rt + jnp.where(first, aw0[:, :LANES], aw1[:, :LANES])
    y0 = jnp.where(first, aw0[:, LANES:] + avrs[0], aw1[:, LANES:] + avrs[1])

    sp = state[...]
    fc = lax.broadcasted_iota(jnp.int32, (c, LANES), 1) < n
    for c0 in range(0, tb, c):
        rows = slice(c0, c0 + c)
        gam = jnp.exp(cl[c0 + c - 1:c0 + c, :])
        gp = jnp.where(same_head, eye_l + _mm(wa[rows], bt[rows], TN), 0.0) * gam
        hp = jnp.where(same_head, _dot(jnp.concatenate([u0[rows], v[rows]], axis=0).astype(BF16),
                                       jnp.concatenate([bt[rows], kt[rows]], axis=0).astype(BF16), TN), 0.0) * gam
        y = _mm(rq[rows], sp, NT) + y0[rows]
        sp = _mm(sp, gp) + hp
        yc = y - _pair_sum(y, fc) * (1.0 / n)
        var = _pair_sum(yc * yc, fc) * (1.0 / n)
        y_ref[rows, :] = yc * lax.rsqrt(var + RWKV_GN_EPS) * ln_w + ln_b + bonus[rows]
    state[...] = sp

    @pl.when(t == pl.num_programs(2) - 1)
    def _():
        sout_ref[0] = state[0:n, 0:n]
        sout_ref[1] = state[n:, n:]


def _wkv(r, lw, k, v, a, vec8, s0, chunk, tb):
    b, l, d = r.shape
    hp = LANES // RWKV_HEAD
    row = pl.BlockSpec((None, tb, LANES), lambda bi, p, t: (bi, t, p))
    st = pl.BlockSpec((None, hp, RWKV_HEAD, RWKV_HEAD), lambda bi, p, t: (bi, p, 0, 0))
    return pl.pallas_call(
        functools.partial(_wkv_body, chunk=chunk),
        grid=(b, d // LANES, l // tb),
        in_specs=[row] * 5 + [pl.BlockSpec((SUBLANES, LANES), lambda bi, p, t: (0, p)), st],
        out_specs=[row, st],
        out_shape=[jax.ShapeDtypeStruct((b, l, d), F32),
                   jax.ShapeDtypeStruct((b, RWKV_HEADS, RWKV_HEAD, RWKV_HEAD), F32)],
        scratch_shapes=[pltpu.VMEM((LANES, LANES), F32)],
        compiler_params=_cparams(("arbitrary", "arbitrary", "arbitrary")),
        name="wkv7_chunked",
    )(r, lw, k, v, a, vec8, s0)


def _pad_rows(a, rows):
    return jnp.pad(a, ((0, rows - a.shape[0]), (0, 0)))


def _mods(m_rows):
    return [t[:, None, :] for t in jnp.split(m_rows, 6, axis=-1)]


def _layer0_weights(w_in, conv_w, a_log, dt_bias, fox_fbias, w_out, ff_w1, ff_w2):
    c = (0, 1536, 2048, 2052, 2056, 2568, 3080, 3592, 3596)
    seg = lambda i: w_in[:, c[i]:c[i + 1]]
    small = jnp.concatenate([seg(2), seg(3), seg(7)], axis=1)
    small = jnp.pad(small, ((0, 0), (0, LANES - small.shape[1])))
    w_all = jnp.concatenate([seg(0), seg(1), seg(4), seg(5), seg(6), small], axis=1).astype(BF16)
    aux = jnp.zeros((SUBLANES, LANES), F32)
    aux = aux.at[0, 4:8].set(a_log).at[1, 4:8].set(dt_bias).at[2, 8:12].set(fox_fbias)
    return dict(w_all=w_all, aux=aux, conv_w8=_pad_rows(conv_w, SUBLANES),
                w_out=w_out.astype(BF16), ff_w1=ff_w1.astype(BF16), ff_w2=ff_w2.astype(BF16))


def _hybrid_layer(x, mods, conv_buf, s0, fox_cache, wts, norm_mix, norm_ff, gdn_onorm, fox_qnorm, fox_knorm,
                  tm, chunk, tb, blk):
    b, l, d = x.shape
    sh1, sc1, gt1, sh2, sc2, gt2 = mods
    row = lambda g: g.reshape(1, -1)
    qkv_raw, z, small, q_b, k_f, v_f, k_b, v_b = _inproj0(
        x, sh1, sc1, row(norm_mix), wts["w_all"], row(fox_qnorm), row(fox_knorm), wts["aux"], tm)
    conv_new = qkv_raw[:, l - (GDN_CONV - 1):, :]
    logf = small[:, :, 8:12]
    conv_init8 = jnp.pad(conv_buf, ((0, 0), (SUBLANES - (GDN_CONV - 1), 0), (0, 0)))
    o_a, s_new = _gdn(qkv_raw, z, small, conv_init8, s0, wts["conv_w8"], row(gdn_onorm), chunk, tb)

    if fox_cache is None:
        lf = jnp.transpose(logf[0]).reshape(FOX_HEADS, l // LANES, LANES)
        c_rows = _cumsum_rows(lf).reshape(FOX_HEADS, l // blk, 1, blk)
        o_b = _fox_prompt(q_b[0], k_b[0], v_b[0], c_rows, blk)[None]
    else:
        k_cache, v_cache, logf_cache = fox_cache
        p = k_cache.shape[1]
        total = p + l
        padded = -(-total // (SUBLANES * LANES)) * (SUBLANES * LANES)
        lf = jnp.concatenate([jnp.transpose(logf_cache, (0, 2, 1)), jnp.transpose(logf, (0, 2, 1))], axis=2)
        lf = jnp.pad(lf, ((0, 0), (0, 0), (padded - total, 0)))
        cs = _cumsum_rows(lf.reshape(b * FOX_HEADS, padded // LANES, LANES)).reshape(b, FOX_HEADS, padded)
        c_cache = cs[:, :, None, padded - total:padded - l]
        c_new = cs[:, :, padded - l:]
        o_b = _fox_step(q_b, k_f, v_f, k_cache.reshape(b, p, FOX_W), v_cache.reshape(b, p, FOX_W),
                        c_cache, c_new[:, :, None, :], c_new[:, :, :, None])

    x = _mix_mlp(_mix0_mlp_body, "l0_out_mlp", x, [o_a, o_b], [gt1, sh2, sc2, gt2], row(norm_ff),
                 wts["w_out"], wts["ff_w1"], wts["ff_w2"], tm)
    return (x, conv_new, s_new, k_f.reshape(b, l, FOX_HEADS, FOX_DH), v_f.reshape(b, l, FOX_HEADS, FOX_DH), logf)


def _layer1_weights(mu, w_r, w_k, w_v, w0, w1, w2, a0, a1, a2, g1, g2, k_k, k_a, r_k, ln_w, ln_b, w_o, ff_w1, ff_w2):
    padc = lambda w: jnp.pad(w, ((0, 0), (0, -w.shape[1] % LANES))).astype(BF16)
    padr = lambda w: jnp.pad(w, ((0, -w.shape[0] % LANES), (0, 0))).astype(BF16)
    proj = [w_r.astype(BF16), w_k.astype(BF16), w_v.astype(BF16),
            padc(w1), padr(w2), padc(a1), padr(a2), padc(g1), padr(g2)]
    return dict(mu8=_pad_rows(mu, SUBLANES), pvec=_pad_rows(jnp.stack([w0, a0]), SUBLANES),
                svec=_pad_rows(jnp.stack([k_k, k_a, r_k.reshape(-1), ln_w, ln_b]), SUBLANES),
                proj=proj, w_o=w_o.astype(BF16), ff_w1=ff_w1.astype(BF16), ff_w2=ff_w2.astype(BF16))


def _rwkv_layer(x, mods, shift_prev, s0, wts, norm_mix, norm_ff, tm, chunk, tb):
    sh1, sc1, gt1, sh2, sc2, gt2 = mods
    row = lambda g: g.reshape(1, -1)
    r, lw, k, v, a, gate, shift_new = _rwkv_proj(
        x, shift_prev[:, None, :], sh1, sc1, row(norm_mix), wts["mu8"], wts["pvec"], wts["proj"], tm)
    y, s_new = _wkv(r, lw, k, v, a, wts["svec"], s0, chunk, tb)
    x = _mix_mlp(_mix1_mlp_body, "l1_out_mlp", x, [y, gate], [gt1, sh2, sc2, gt2], row(norm_ff),
                 wts["w_o"], wts["ff_w1"], wts["ff_w2"], tm)
    return x, shift_new[:, 0, :], s_new


def kernel(x_prompt, x_sample, c_prompt, c_sample, cache_l0_conv, state_l0_delta, cache_l0_fox_k, cache_l0_fox_v, cache_l0_fox_logf, state_l1_shift, state_l1_wkv, l0_ada_w, l0_ada_b, l0_norm_mix, l0_norm_ff, l0_w_in, l0_conv_w, l0_a_log, l0_dt_bias, l0_gdn_onorm, l0_fox_qnorm, l0_fox_knorm, l0_fox_fbias, l0_w_out, l0_ff_w1, l0_ff_w2, l1_ada_w, l1_ada_b, l1_norm_mix, l1_norm_ff, l1_mu, l1_w_r, l1_w_k, l1_w_v, l1_w0, l1_w1, l1_w2, l1_a0, l1_a1, l1_a2, l1_g1, l1_g2, l1_k_k, l1_k_a, l1_r_k, l1_ln_w, l1_ln_b, l1_w_o, l1_ff_w1, l1_ff_w2):
    bp, lp, d = x_prompt.shape
    bs, ls, _ = x_sample.shape
    c_all = _pad_rows(jnp.concatenate([c_prompt, c_sample], axis=0), 2 * SUBLANES)
    m0 = _ada(c_all, l0_ada_w, l0_ada_b)
    m1 = _ada(c_all, l1_ada_w, l1_ada_b)
    mods0_p, mods0_s = _mods(m0[:bp]), _mods(m0[bp:bp + bs])
    mods1_p, mods1_s = _mods(m1[:bp]), _mods(m1[bp:bp + bs])

    w0 = _layer0_weights(l0_w_in, l0_conv_w, l0_a_log, l0_dt_bias, l0_fox_fbias, l0_w_out, l0_ff_w1, l0_ff_w2)
    w1 = _layer1_weights(l1_mu, l1_w_r, l1_w_k, l1_w_v, l1_w0, l1_w1, l1_w2, l1_a0, l1_a1, l1_a2, l1_g1, l1_g2,
                         l1_k_k, l1_k_a, l1_r_k, l1_ln_w, l1_ln_b, l1_w_o, l1_ff_w1, l1_ff_w2)
    norms0 = (l0_norm_mix, l0_norm_ff, l0_gdn_onorm, l0_fox_qnorm, l0_fox_knorm)

    tm_p = min(512, lp)
    y_p, conv_p, delta_p, fox_k_p, fox_v_p, fox_logf_p = _hybrid_layer(
        x_prompt, mods0_p, jnp.zeros((bp, GDN_CONV - 1, GDN_QKV), F32),
        jnp.zeros((bp, GDN_HEADS, GDN_DK, GDN_DV), F32), None, w0, *norms0,
        tm=tm_p, chunk=min(GDN_CHUNK, lp), tb=min(256, lp), blk=min(512, lp))
    y_s, conv_s, delta_s, fox_k_s, fox_v_s, fox_logf_s = _hybrid_layer(
        x_sample, mods0_s, cache_l0_conv, state_l0_delta,
        (cache_l0_fox_k, cache_l0_fox_v, cache_l0_fox_logf), w0, *norms0,
        tm=ls, chunk=min(GDN_CHUNK, ls), tb=ls, blk=None)

    y_p, shift_p, wkv_p = _rwkv_layer(
        y_p, mods1_p, jnp.zeros((bp, d), F32), jnp.zeros((bp, RWKV_HEADS, RWKV_HEAD, RWKV_HEAD), F32),
        w1, l1_norm_mix, l1_norm_ff, tm=tm_p, chunk=min(RWKV_CHUNK, lp), tb=min(256, lp))
    y_s, shift_s, wkv_s = _rwkv_layer(
        y_s, mods1_s, state_l1_shift, state_l1_wkv, w1, l1_norm_mix, l1_norm_ff,
        tm=ls, chunk=min(RWKV_CHUNK, ls), tb=ls)

    return (y_p, y_s, conv_p, conv_s, delta_p, delta_s, fox_k_p, fox_k_s,
            fox_v_p, fox_v_s, fox_logf_p, fox_logf_s, shift_p, shift_s, wkv_p, wkv_s)
```

```python
import functools
import math

import jax
import jax.numpy as jnp
from jax import lax
from jax.experimental import pallas as pl
from jax.experimental.pallas import tpu as pltpu

F32 = jnp.float32
BF16 = jnp.bfloat16

D_MODEL = 1024
D_FF = 4 * D_MODEL
NORM_EPS = 1e-6
LOG2E = 1.4426950408889634

GDN_HEADS = 4
GDN_DK = 128
GDN_DV = 128
GDN_CONV = 4
GDN_QKV = GDN_HEADS * (2 * GDN_DK + GDN_DV)
GDN_CHUNK = 64
SOLVE_HI_LEVELS = 3

FOX_HEADS = 4
FOX_DH = 128
FOX_W = FOX_HEADS * FOX_DH
FOX_SPLIT = 2

RWKV_HEAD = 64
RWKV_HEADS = D_MODEL // RWKV_HEAD
RWKV_GN_EPS = 1e-5 * RWKV_HEAD
RWKV_CHUNK = 64
WKV_PAIRS = 2

LANES = 128
SUBLANES = 8
VMEM_LIMIT = 56 * 1024 * 1024

NN = (((1,), (0,)), ((), ()))
NT = (((1,), (1,)), ((), ()))
TN = (((0,), (0,)), ((), ()))


def _dot(a, b, dims=NN):
    return lax.dot_general(a, b, dims, preferred_element_type=F32)


def _mm(a, b, dims=NN):
    return _dot(a.astype(BF16), b.astype(BF16), dims)


def _split2(x):
    hi = x.astype(BF16)
    lo = (x - hi.astype(F32)).astype(BF16)
    return hi, lo


def _split3(x):
    hi = x.astype(BF16)
    r = x - hi.astype(F32)
    mid = r.astype(BF16)
    lo = (r - mid.astype(F32)).astype(BF16)
    return hi, mid, lo


def _mm3(a, b, dims=NN):
    ah, al = _split2(a)
    bh, bl = _split2(b)
    return _dot(ah, bh, dims) + (_dot(ah, bl, dims) + _dot(al, bh, dims))


def _mask_mm(mask01, x):
    x1, x2, x3 = _split3(x)
    return _dot(mask01, x1) + (_dot(mask01, x2) + _dot(mask01, x3))


def _sigmoid(x):
    return 1.0 / (1.0 + jnp.exp(-x))


def _softplus(x):
    return jnp.maximum(x, 0.0) + jnp.log(1.0 + jnp.exp(-jnp.abs(x)))


def _silu(x):
    return x * _sigmoid(x)


def _rms_mod(x, g, sc, sh):
    ms = jnp.mean(x * x, axis=-1, keepdims=True)
    return (x * lax.rsqrt(ms + NORM_EPS) * g) * (1.0 + sc) + sh


def _tri_masks(c):
    row = lax.broadcasted_iota(jnp.int32, (c, c), 0)
    col = lax.broadcasted_iota(jnp.int32, (c, c), 1)
    return row >= col, row > col, row == col


def _cparams(sem):
    return pltpu.CompilerParams(dimension_semantics=sem, vmem_limit_bytes=VMEM_LIMIT)


def _const_spec(shape):
    nd = len(shape)
    return pl.BlockSpec(shape, lambda *_: (0,) * nd, pipeline_mode=pl.Buffered(1))


def _ada_body(c_ref, w_ref, b_ref, o_ref):
    o_ref[...] = _mm3(_silu(c_ref[...]), w_ref[...]) + b_ref[...]


def _ada(c_all, w, b):
    rows, d = c_all.shape
    n = w.shape[1]
    tn = 512
    return pl.pallas_call(
        _ada_body,
        grid=(n // tn,),
        in_specs=[pl.BlockSpec((rows, d), lambda j: (0, 0)),
                  pl.BlockSpec((d, tn), lambda j: (0, j)),
                  pl.BlockSpec((1, tn), lambda j: (0, j))],
        out_specs=pl.BlockSpec((rows, tn), lambda j: (0, j)),
        out_shape=jax.ShapeDtypeStruct((rows, n), F32),
        compiler_params=_cparams(("arbitrary",)),
        name="ada_ln",
    )(c_all, w, b.reshape(1, n))


_IN0_QKV = (0, GDN_QKV)
_IN0_Z = (GDN_QKV, GDN_QKV + 512)
_IN0_Q = (2048, 2560)
_IN0_K = (2560, 3072)
_IN0_V = (3072, 3584)
_IN0_S = (3584, 3712)
_IN0_COLS = 3712


def _inproj0_body(x_ref, sh_ref, sc_ref, g_ref, w_ref, qn_ref, kn_ref, aux_ref,
                  qkv_ref, z_ref, small_ref, qb_ref, kf_ref, vf_ref, kb_ref, vb_ref, ka_ref, cref_ref, carry):
    h = _rms_mod(x_ref[...], g_ref[...], sc_ref[...], sh_ref[...]).astype(BF16)
    qkv_ref[...] = _dot(h, w_ref[:, _IN0_QKV[0]:_IN0_QKV[1]])
    z_ref[...] = _dot(h, w_ref[:, _IN0_Z[0]:_IN0_Z[1]])
    q = _dot(h, w_ref[:, _IN0_Q[0]:_IN0_Q[1]])
    k = _dot(h, w_ref[:, _IN0_K[0]:_IN0_K[1]])
    v = _dot(h, w_ref[:, _IN0_V[0]:_IN0_V[1]])
    raw = _dot(h, w_ref[:, _IN0_S[0]:_IN0_S[1]])
    qn = qn_ref[...]
    kn = kn_ref[...]
    for hd in range(FOX_HEADS):
        sl = slice(hd * FOX_DH, (hd + 1) * FOX_DH)
        qh = q[:, sl]
        kh = k[:, sl]
        qh = qh * lax.rsqrt(jnp.mean(qh * qh, axis=-1, keepdims=True) + NORM_EPS) * qn * (FOX_DH ** -0.5 * LOG2E)
        kh = kh * lax.rsqrt(jnp.mean(kh * kh, axis=-1, keepdims=True) + NORM_EPS) * kn
        qb_ref[:, sl] = qh.astype(BF16)
        kf_ref[:, sl] = kh
        kb_ref[:, sl] = kh.astype(BF16)
    vf_ref[...] = v
    vb_ref[...] = v.astype(BF16)
    aux = aux_ref[...]
    a_log, dt_bias, f_bias = aux[0:1], aux[1:2], aux[2:3]
    lane = lax.broadcasted_iota(jnp.int32, raw.shape, 1)
    beta = _sigmoid(raw)
    gdec = -jnp.exp(a_log) * _softplus(raw + dt_bias)
    logf = -_softplus(-(raw + f_bias))
    small_ref[...] = jnp.where(lane < 4, beta, jnp.where(lane < 8, gdec, jnp.where(lane < 12, logf, 0.0)))

    @pl.when(pl.program_id(1) == 0)
    def _():
        carry[...] = jnp.zeros_like(carry)

    tm = raw.shape[0]
    tri, _, _ = _tri_masks(tm)
    within = _mask_mm(tri.astype(BF16), logf * LOG2E)
    cref_ref[...] = carry[...] + within[0:1, :]
    carry[...] = carry[...] + within[tm - 1:tm, :]
    bias = within[0:1, :] - within
    lane_h = lax.broadcasted_iota(jnp.int32, (tm, FOX_DH), 1)
    for hd in range(FOX_HEADS):
        col = bias[:, 8 + hd:9 + hd]
        b1 = col.astype(BF16).astype(F32)
        b2 = (col - b1).astype(BF16).astype(F32)
        b3 = col - b1 - b2
        cols = jnp.where(lane_h == 0, b1, jnp.where(lane_h == 1, b2, jnp.where(lane_h == 2, b3, 0.0)))
        ka_ref[:, hd * FOX_DH:(hd + 1) * FOX_DH] = cols.astype(BF16)


def _inproj0(x, sh, sc, g, w_all, qn, kn, aux, tm):
    b, l, d = x.shape
    row = lambda width: pl.BlockSpec((None, tm, width), lambda bi, i: (bi, i, 0))
    mod = pl.BlockSpec((None, 1, d), lambda bi, i: (bi, 0, 0))
    out = lambda width, dt: jax.ShapeDtypeStruct((b, l, width), dt)
    return pl.pallas_call(
        _inproj0_body,
        grid=(b, l // tm),
        in_specs=[row(d), mod, mod, _const_spec((1, d)), _const_spec((d, _IN0_COLS)),
                  _const_spec((1, FOX_DH)), _const_spec((1, FOX_DH)), _const_spec((SUBLANES, LANES))],
        out_specs=[row(GDN_QKV), row(512), row(LANES), row(FOX_W), row(FOX_W), row(FOX_W), row(FOX_W), row(FOX_W),
                   row(FOX_W), pl.BlockSpec((None, None, 1, LANES), lambda bi, i: (bi, i, 0, 0))],
        out_shape=[out(GDN_QKV, F32), out(512, F32), out(LANES, F32), out(FOX_W, BF16),
                   out(FOX_W, F32), out(FOX_W, F32), out(FOX_W, BF16), out(FOX_W, BF16), out(FOX_W, BF16),
                   jax.ShapeDtypeStruct((b, l // tm, 1, LANES), F32)],
        scratch_shapes=[pltpu.VMEM((1, LANES), F32)],
        compiler_params=_cparams(("arbitrary", "arbitrary")),
        name="l0_in_proj",
    )(x, sh, sc, g, w_all, qn, kn, aux)


def _chunk_masks(tb, chunk):
    shift = int(math.log2(chunk))
    row = lax.broadcasted_iota(jnp.int32, (tb, tb), 0)
    col = lax.broadcasted_iota(jnp.int32, (tb, tb), 1)
    same = (row >> shift) == (col >> shift)
    return same & (row >= col), same & (row > col), row == col


def _gdn_body(qkv_ref, z_ref, small_ref, cinit_ref, s0_ref, cw_ref, on_ref,
              o_ref, sout_ref, ext, state, *, chunk):
    c = chunk
    tb = qkv_ref.shape[0]
    ci = pl.program_id(1)

    @pl.when(ci == 0)
    def _():
        ext[0:SUBLANES, :] = cinit_ref[...]
        state[...] = s0_ref[...]

    ext[SUBLANES:SUBLANES + tb, :] = qkv_ref[...]
    cw = cw_ref[...]
    base = SUBLANES - (GDN_CONV - 1)
    y = ext[base:base + tb, :] * cw[0:1]
    for i in range(1, GDN_CONV):
        y = y + ext[base + i:base + i + tb, :] * cw[i:i + 1]
    qkv = _silu(y)
    ext[0:SUBLANES, :] = ext[tb:tb + SUBLANES, :]

    tri, strict, eye = _chunk_masks(tb, c)
    sm = small_ref[...]
    gc_all = _mask_mm(tri.astype(BF16), sm)
    onorm = on_ref[...]
    eye_d = _tri_masks(GDN_DK)[2]
    heads = []
    for hd in range(GDN_HEADS):
        q = qkv[:, hd * GDN_DK:(hd + 1) * GDN_DK]
        k = qkv[:, 512 + hd * GDN_DK:512 + (hd + 1) * GDN_DK]
        v = qkv[:, 1024 + hd * GDN_DV:1024 + (hd + 1) * GDN_DV]
        q = q * lax.rsqrt(jnp.sum(q * q, axis=-1, keepdims=True) + NORM_EPS) * GDN_DK ** -0.5
        k = k * lax.rsqrt(jnp.sum(k * k, axis=-1, keepdims=True) + NORM_EPS)
        beta = sm[:, hd:hd + 1]
        gc = gc_all[:, 4 + hd:5 + hd]
        gc_row = jnp.sum(jnp.where(eye, gc, 0.0), axis=0, keepdims=True)
        dec = jnp.where(tri, jnp.exp(jnp.where(tri, gc - gc_row, 0.0)), 0.0)
        kb = k.astype(BF16)
        egc = jnp.exp(gc)
        heads.append(dict(k=k, qe=q * egc,
                          n=-jnp.where(strict, beta * _dot(kb, kb, NT) * dec, 0.0),
                          sol=jnp.concatenate([k * (beta * egc), v * beta], axis=1),
                          qk=_dot(q.astype(BF16), kb, NT) * dec))
    levels = max(1, int(math.ceil(math.log2(c))))
    for lvl in range(levels):
        mm = _mm3 if lvl < SOLVE_HI_LEVELS else _mm
        for x in heads:
            x["sol"] = x["sol"] + mm(x["n"], x["sol"])
        if lvl + 1 < levels:
            for x in heads:
                x["n"] = mm(x["n"], x["n"])
    for x in heads:
        qs = _mm(x["qk"], x["sol"])
        x["oq"] = x["qe"] - qs[:, :GDN_DK]
        x["o0"] = qs[:, GDN_DK:]
    s_hs = [state[hd] for hd in range(GDN_HEADS)]
    for c0 in range(0, tb, c):
        rows = slice(c0, c0 + c)
        for hd, x in enumerate(heads):
            g_last = gc_all[c0 + c - 1:c0 + c, 4 + hd:5 + hd]
            ks = _mm3(x["k"][rows] * jnp.exp(g_last - gc_all[rows, 4 + hd:5 + hd]), x["sol"][rows], TN)
            o = _mm3(x["oq"][rows], s_hs[hd]) + x["o0"][rows]
            s_hs[hd] = _mm3(jnp.where(eye_d, jnp.exp(g_last), 0.0) - ks[:, :GDN_DK], s_hs[hd]) + ks[:, GDN_DK:]
            zz = z_ref[rows, hd * GDN_DV:(hd + 1) * GDN_DV]
            o = o * lax.rsqrt(jnp.mean(o * o, axis=-1, keepdims=True) + NORM_EPS) * onorm
            o_ref[rows, hd * GDN_DV:(hd + 1) * GDN_DV] = (o * _silu(zz)).astype(BF16)
    for hd in range(GDN_HEADS):
        state[hd] = s_hs[hd]

    @pl.when(ci == pl.num_programs(1) - 1)
    def _():
        sout_ref[...] = state[...]


def _gdn(qkv_raw, z, small, conv_init8, s0, conv_w8, onorm, chunk, tb):
    b, l, _ = qkv_raw.shape
    row = lambda width: pl.BlockSpec((None, tb, width), lambda bi, i: (bi, i, 0))
    st = pl.BlockSpec((None, GDN_HEADS, GDN_DK, GDN_DV), lambda bi, i: (bi, 0, 0, 0))
    return pl.pallas_call(
        functools.partial(_gdn_body, chunk=chunk),
        grid=(b, l // tb),
        in_specs=[row(GDN_QKV), row(512), row(LANES),
                  pl.BlockSpec((None, SUBLANES, GDN_QKV), lambda bi, i: (bi, 0, 0)), st,
                  _const_spec((SUBLANES, GDN_QKV)), _const_spec((1, GDN_DV))],
        out_specs=[row(512), st],
        out_shape=[jax.ShapeDtypeStruct((b, l, 512), BF16),
                   jax.ShapeDtypeStruct((b, GDN_HEADS, GDN_DK, GDN_DV), F32)],
        scratch_shapes=[pltpu.VMEM((tb + SUBLANES, GDN_QKV), F32),
                        pltpu.VMEM((GDN_HEADS, GDN_DK, GDN_DV), F32)],
        compiler_params=_cparams(("arbitrary", "arbitrary")),
        name="gdn_mixer",
    )(qkv_raw, z, small, conv_init8, s0, conv_w8, onorm)


def _cumsum_body(x_ref, o_ref):
    x = x_ref[...]
    rows = x.shape[0]
    li = lax.broadcasted_iota(jnp.int32, (LANES, LANES), 0)
    lj = lax.broadcasted_iota(jnp.int32, (LANES, LANES), 1)
    upper = (li <= lj).astype(BF16)
    ones = jnp.ones((LANES, LANES), BF16)
    ri = lax.broadcasted_iota(jnp.int32, (rows, rows), 0)
    rj = lax.broadcasted_iota(jnp.int32, (rows, rows), 1)
    before = (rj < ri).astype(BF16)
    x1, x2, x3 = _split3(x)
    within = _dot(x1, upper) + (_dot(x2, upper) + _dot(x3, upper))
    total = _dot(x1, ones) + (_dot(x2, ones) + _dot(x3, ones))
    o_ref[...] = within + _mask_mm(before, total)


def _cumsum_rows(x):
    g, rows, _ = x.shape
    spec = pl.BlockSpec((None, rows, LANES), lambda i: (i, 0, 0))
    return pl.pallas_call(
        _cumsum_body, grid=(g,), in_specs=[spec], out_specs=spec,
        out_shape=jax.ShapeDtypeStruct(x.shape, F32),
        compiler_params=_cparams(("arbitrary",)),
        name="logf_cumsum",
    )(x)


def _fox_prompt_body(cref_ref, q_ref, k_ref, ka_ref, v_ref, o_ref, s_a, s_b, m_scr, acc_scr, *, blk):
    hd = pl.program_id(0)
    qi = pl.program_id(1)
    sub = blk // FOX_SPLIT
    half = blk // 2
    lane = lax.broadcasted_iota(jnp.int32, (blk, FOX_DH), 1)
    ones = jnp.ones((half, FOX_DH), BF16)
    q = jnp.concatenate([q_ref[...], jnp.where(lane < 3, 1.0, 0.0).astype(BF16)], axis=1)
    c0 = cref_ref[hd, qi]
    row = lax.broadcasted_iota(jnp.int32, (sub, half), 0)
    col = lax.broadcasted_iota(jnp.int32, (sub, half), 1)
    m_scr[...] = jnp.full_like(m_scr, -jnp.inf)
    acc_scr[...] = jnp.zeros_like(acc_scr)

    def scores(h, s_ref):
        rows = pl.ds(pl.multiple_of(h * half, half), half)
        kh = jnp.concatenate([k_ref[rows, :], ka_ref[rows, :]], axis=1)
        for i in range(FOX_SPLIT):
            s_ref[i * sub:(i + 1) * sub, :] = _dot(q[i * sub:(i + 1) * sub], kh, NT)

    def consume(h, s_ref, delta, first_col):
        rows = pl.ds(pl.multiple_of(h * half, half), half)
        vh = jnp.concatenate([v_ref[rows, :], ones], axis=1)
        for i in range(FOX_SPLIT):
            rs = slice(i * sub, (i + 1) * sub)
            s = s_ref[rs, :]
            if first_col is not None:
                s = jnp.where(col + first_col <= row + i * sub, s, -jnp.inf)
            m = m_scr[rs, :]
            m_new = jnp.maximum(m, jnp.max(s, axis=-1, keepdims=True) + delta)
            shift = m_new - delta
            p = jnp.concatenate([jnp.exp2(s[:, g * LANES:(g + 1) * LANES] - shift) for g in range(half // LANES)],
                                axis=1)
            pv = _dot(p.astype(BF16), vh)
            alpha = jnp.exp2(m - m_new)
            acc = acc_scr[rs, :]
            acc_scr[rs, :] = jnp.concatenate(
                [alpha * acc[:, g * LANES:(g + 1) * LANES] + pv[:, g * LANES:(g + 1) * LANES] for g in range(2)],
                axis=1)
            m_scr[rs, :] = m_new

    def step(j, carry):
        delta = c0 - cref_ref[hd, j]
        scores(2 * j + 1, s_b)
        consume(2 * j, s_a, delta, None)
        scores(2 * j + 2, s_a)
        consume(2 * j + 1, s_b, delta, None)
        return carry

    scores(0, s_a)
    lax.fori_loop(0, qi, step, 0)
    scores(2 * qi + 1, s_b)
    consume(2 * qi, s_a, 0.0, 0)
    consume(2 * qi + 1, s_b, 0.0, half)
    acc = acc_scr[...]
    o_ref[...] = (acc[:, :FOX_DH] / acc[:, FOX_DH:FOX_DH + 1]).astype(BF16)


def _fox_prompt(q, k, ka, v, cref, blk):
    l = q.shape[0]
    nb = l // blk
    res = pl.BlockSpec((l, FOX_DH), lambda h, i: (0, h))
    return pl.pallas_call(
        functools.partial(_fox_prompt_body, blk=blk),
        grid=(FOX_HEADS, nb),
        in_specs=[pl.BlockSpec(memory_space=pltpu.SMEM),
                  pl.BlockSpec((blk, FOX_DH), lambda h, i: (i, h)), res, res, res],
        out_specs=pl.BlockSpec((blk, FOX_DH), lambda h, i: (i, h)),
        out_shape=jax.ShapeDtypeStruct((l, FOX_W), BF16),
        scratch_shapes=[pltpu.VMEM((blk, blk // 2), F32), pltpu.VMEM((blk, blk // 2), F32),
                        pltpu.VMEM((blk, LANES), F32), pltpu.VMEM((blk, 2 * FOX_DH), F32)],
        compiler_params=_cparams(("arbitrary", "arbitrary")),
        name="fox_prompt_attn",
    )(cref, q, k, ka, v)


def _fox_step_body(q_ref, kn_ref, vn_ref, kc_ref, vc_ref, cc_ref, cnr_ref, cnc_ref, o_ref):
    q = q_ref[...]
    n = q.shape[0]
    c_i = cnc_ref[...]
    s_c = _mm(q, kc_ref[...], NT) + (c_i - cc_ref[...]) * LOG2E
    s_n = _mm(q, kn_ref[...], NT) + (c_i - cnr_ref[...]) * LOG2E
    tri, _, _ = _tri_masks(n)
    s_n = jnp.where(tri, s_n, -jnp.inf)
    m = jnp.maximum(jnp.max(s_c, axis=-1, keepdims=True), jnp.max(s_n, axis=-1, keepdims=True))
    p_c = jnp.exp2(s_c - m)
    p_n = jnp.exp2(s_n - m)
    l = jnp.sum(p_c, axis=-1, keepdims=True) + jnp.sum(p_n, axis=-1, keepdims=True)
    o = _mm(p_c, vc_ref[...]) + _mm(p_n, vn_ref[...])
    o_ref[...] = (o / l).astype(BF16)


def _fox_step(q, k_new, v_new, k_cache, v_cache, c_cache, c_new_row, c_new_col):
    b, n, _ = q.shape
    p = k_cache.shape[1]
    new = pl.BlockSpec((None, n, FOX_DH), lambda bi, h: (bi, 0, h))
    cache = pl.BlockSpec((None, p, FOX_DH), lambda bi, h: (bi, 0, h))
    return pl.pallas_call(
        _fox_step_body,
        grid=(b, FOX_HEADS),
        in_specs=[new, new, new, cache, cache,
                  pl.BlockSpec((None, None, 1, p), lambda bi, h: (bi, h, 0, 0)),
                  pl.BlockSpec((None, None, 1, n), lambda bi, h: (bi, h, 0, 0)),
                  pl.BlockSpec((None, None, n, 1), lambda bi, h: (bi, h, 0, 0))],
        out_specs=new,
        out_shape=jax.ShapeDtypeStruct((b, n, FOX_W), BF16),
        compiler_params=_cparams(("arbitrary", "arbitrary")),
        name="fox_step_attn",
    )(q, k_new, v_new, k_cache, v_cache, c_cache, c_new_row, c_new_col)


def _resid_mlp(x, mix, gt1, sh2, sc2, gt2, g2, w1_ref, w2_ref):
    x1 = x + gt1 * mix
    h2 = _rms_mod(x1, g2, sc2, sh2).astype(BF16)
    acc = jnp.zeros_like(x1)
    step = D_MODEL
    for c0 in range(0, D_FF, step):
        hid = jnp.maximum(_dot(h2, w1_ref[:, c0:c0 + step]), 0.0)
        acc = acc + _dot((hid * hid).astype(BF16), w2_ref[c0:c0 + step, :])
    return x1 + gt2 * acc


def _mix0_mlp_body(x_ref, oa_ref, ob_ref, gt1, sh2, sc2, gt2, g2, wo_ref, w1_ref, w2_ref, out_ref):
    half = GDN_HEADS * GDN_DV
    mix = _dot(oa_ref[...], wo_ref[0:half, :]) + _dot(ob_ref[...], wo_ref[half:, :])
    out_ref[...] = _resid_mlp(x_ref[...], mix, gt1[...], sh2[...], sc2[...], gt2[...], g2[...], w1_ref, w2_ref)


def _mix1_mlp_body(x_ref, y_ref, gate_ref, gt1, sh2, sc2, gt2, g2, wo_ref, w1_ref, w2_ref, out_ref):
    mix = _mm(y_ref[...] * gate_ref[...], wo_ref[...])
    out_ref[...] = _resid_mlp(x_ref[...], mix, gt1[...], sh2[...], sc2[...], gt2[...], g2[...], w1_ref, w2_ref)


def _mix_mlp(body, name, x, mix_in, mods, g2, wo, w1, w2, tm):
    b, l, d = x.shape
    row = lambda width: pl.BlockSpec((None, tm, width), lambda bi, i: (bi, i, 0))
    mod = pl.BlockSpec((None, 1, d), lambda bi, i: (bi, 0, 0))
    return pl.pallas_call(
        body,
        grid=(b, l // tm),
        in_specs=[row(d)] + [row(a.shape[-1]) for a in mix_in] + [mod] * 4
                 + [_const_spec((1, d)), _const_spec(wo.shape), _const_spec(w1.shape), _const_spec(w2.shape)],
        out_specs=row(d),
        out_shape=jax.ShapeDtypeStruct((b, l, d), F32),
        compiler_params=_cparams(("arbitrary", "arbitrary")),
        name=name,
    )(x, *mix_in, *mods, g2, wo, w1, w2)


def _rwkv_proj_body(x_ref, shp_ref, sh_ref, sc_ref, g_ref, mu_ref, vec_ref,
                    wr_ref, wk_ref, wv_ref, w1_ref, w2_ref, a1_ref, a2_ref, g1_ref, g2_ref,
                    r_ref, lw_ref, k_ref, v_ref, a_ref, gate_ref, shift_ref, hext):
    tm = x_ref.shape[0]
    i = pl.program_id(1)
    h = _rms_mod(x_ref[...], g_ref[...], sc_ref[...], sh_ref[...])

    @pl.when(i == 0)
    def _():
        hext[SUBLANES - 1:SUBLANES, :] = shp_ref[...]

    hext[SUBLANES:SUBLANES + tm, :] = h
    h_prev = hext[SUBLANES - 1:SUBLANES - 1 + tm, :]
    hext[SUBLANES - 1:SUBLANES, :] = h[tm - 1:tm, :]
    shift_ref[...] = h[tm - 1:tm, :]

    xx = h_prev - h
    mu = mu_ref[...]
    vec = vec_ref[...]
    w0, a0 = vec[0:1], vec[1:2]
    mixed = lambda j: (h + xx * mu[j:j + 1]).astype(BF16)
    r_ref[...] = _dot(mixed(0), wr_ref[...])
    wl = w0 + _mm(jnp.tanh(_dot(mixed(1), w1_ref[...])), w2_ref[...])
    lw_ref[...] = -jnp.exp(-_softplus(-wl) - 0.5)
    k_ref[...] = _dot(mixed(2), wk_ref[...])
    v_ref[...] = _dot(mixed(3), wv_ref[...])
    a_ref[...] = _sigmoid(a0 + _mm(_dot(mixed(4), a1_ref[...]), a2_ref[...]))
    gate_ref[...] = _mm(_sigmoid(_dot(mixed(5), g1_ref[...])), g2_ref[...])


def _rwkv_proj(x, shift_prev, sh, sc, g, mu8, vec8, weights, tm):
    b, l, d = x.shape
    row = pl.BlockSpec((None, tm, d), lambda bi, i: (bi, i, 0))
    mod = pl.BlockSpec((None, 1, d), lambda bi, i: (bi, 0, 0))
    big = jax.ShapeDtypeStruct((b, l, d), F32)
    return pl.pallas_call(
        _rwkv_proj_body,
        grid=(b, l // tm),
        in_specs=[row, mod, mod, mod, _const_spec((1, d)), _const_spec((SUBLANES, d)), _const_spec((SUBLANES, d))]
                 + [_const_spec(w.shape) for w in weights],
        out_specs=[row] * 6 + [mod],
        out_shape=[big] * 6 + [jax.ShapeDtypeStruct((b, 1, d), F32)],
        scratch_shapes=[pltpu.VMEM((tm + SUBLANES, d), F32)],
        compiler_params=_cparams(("arbitrary", "arbitrary")),
        name="rwkv_proj",
    )(x, shift_prev, sh, sc, g, mu8, vec8, *weights)


def _pair_sum(x, first):
    lo = jnp.sum(jnp.where(first, x, 0.0), axis=-1, keepdims=True)
    hi = jnp.sum(jnp.where(first, 0.0, x), axis=-1, keepdims=True)
    return jnp.where(first, lo, hi)


def _wkv_body(r_ref, lw_ref, k_ref, v_ref, a_ref, vec_ref, s0_ref, y_ref, sout_ref, state, *, chunk):
    c = chunk
    tb = r_ref.shape[0]
    n = RWKV_HEAD
    t = pl.program_id(2)
    pairs = r_ref.shape[1] // LANES

    @pl.when(t == 0)
    def _():
        state[...] = jnp.zeros_like(state)
        for p in range(pairs):
            state[p, 0:n, 0:n] = s0_ref[2 * p]
            state[p, n:, n:] = s0_ref[2 * p + 1]

    tri, strict, eye = _chunk_masks(tb, c)
    first = lax.broadcasted_iota(jnp.int32, (tb, LANES), 1) < n
    hr = lax.broadcasted_iota(jnp.int32, (LANES, LANES), 0)
    hc = lax.broadcasted_iota(jnp.int32, (LANES, LANES), 1)
    same_head = (hr < n) == (hc < n)
    eye_l = jnp.where(hr == hc, 1.0, 0.0)

    fc = lax.broadcasted_iota(jnp.int32, (c, LANES), 1) < n
    levels = max(1, int(math.ceil(math.log2(c))))
    lanes = [slice(p * LANES, (p + 1) * LANES) for p in range(pairs)]

    ctx = [_wkv_scores(r_ref[:, ls], lw_ref[:, ls], k_ref[:, ls], v_ref[:, ls], a_ref[:, ls], vec_ref[:, ls],
                       c, tri, strict, eye, first) for ls in lanes]
    for _ in range(1, levels):
        for x in ctx:
            x["nks"] = [_mm(nk, nk) for nk in x["nks"]]
        for x in ctx:
            x["t_invs"] = [t_inv + _mm(nk, t_inv) for nk, t_inv in zip(x["nks"], x["t_invs"])]
    for x in ctx:
        x["ws"] = [_mm(t_inv, rhs) for t_inv, rhs in zip(x["t_invs"], x["rhs"])]
    for x in ctx:
        _wkv_state_free_terms(x, first)

    sps = [state[p] for p in range(pairs)]
    for c0 in range(0, tb, c):
        rows = slice(c0, c0 + c)
        for p, x in enumerate(ctx):
            gam = jnp.exp(x["cl"][c0 + c - 1:c0 + c, :])
            gp = jnp.where(same_head, eye_l + _mm(x["wa"][rows], x["bt"][rows], TN), 0.0) * gam
            hp = jnp.where(same_head,
                           _dot(jnp.concatenate([x["u0"][rows], x["v"][rows]], axis=0).astype(BF16),
                                jnp.concatenate([x["bt"][rows], x["kt"][rows]], axis=0).astype(BF16), TN), 0.0) * gam
            y = _mm(x["rq"][rows], sps[p], NT) + x["y0"][rows]
            sps[p] = _mm(sps[p], gp) + hp
            yc = y - _pair_sum(y, fc) * (1.0 / n)
            var = _pair_sum(yc * yc, fc) * (1.0 / n)
            y_ref[rows, lanes[p]] = yc * lax.rsqrt(var + RWKV_GN_EPS) * x["ln_w"] + x["ln_b"] + x["bonus"][rows]
    for p in range(pairs):
        state[p] = sps[p]

    @pl.when(t == pl.num_programs(2) - 1)
    def _():
        for p in range(pairs):
            sout_ref[2 * p] = state[p, 0:n, 0:n]
            sout_ref[2 * p + 1] = state[p, n:, n:]


def _wkv_scores(r, lw, k, v, a, vec, c, tri, strict, eye, first):
    tb = r.shape[0]
    k_k, k_a, r_k, ln_w, ln_b = vec[0:1], vec[1:2], vec[2:3], vec[3:4], vec[4:5]
    kq = k * k_k
    kkn = kq * lax.rsqrt(_pair_sum(kq * kq, first) + NORM_EPS)
    kp = k * (1.0 + (a - 1.0) * k_a)
    cl = _mask_mm(tri.astype(BF16), lw)
    e_neg = jnp.exp(-cl)
    at = -kkn * jnp.exp(cl - lw)
    rt = r * jnp.exp(cl)
    bt = (kkn * a) * e_neg
    kt = kp * e_neg
    kcat = jnp.concatenate([bt, kt], axis=0).astype(BF16)
    vb = v.astype(BF16)
    x = dict(cl=cl, rt=rt, bt=bt, kt=kt, v=v, ln_w=ln_w, ln_b=ln_b,
             bonus=_pair_sum(r * kp * r_k, first) * v, arbs=[], avrs=[], nks=[], t_invs=[], rhs=[])
    for head_lanes in (first, jnp.logical_not(first)):
        at_e = jnp.where(head_lanes, at, 0.0)
        ma = _dot(at_e.astype(BF16), kcat, NT)
        mr = _dot(jnp.where(head_lanes, rt, 0.0).astype(BF16), kcat, NT)
        nk = jnp.where(strict, ma[:, :tb], 0.0)
        av_a = _dot(jnp.where(strict, ma[:, tb:], 0.0).astype(BF16), vb)
        x["avrs"].append(_dot(jnp.where(tri, mr[:, tb:], 0.0).astype(BF16), vb))
        x["arbs"].append(jnp.where(tri, mr[:, :tb], 0.0).astype(BF16))
        x["nks"].append(nk)
        x["t_invs"].append(jnp.where(eye, 1.0, nk))
        x["rhs"].append(jnp.concatenate([at_e, av_a], axis=1))
    return x


def _wkv_state_free_terms(x, first):
    ws, arbs, avrs = x["ws"], x["arbs"], x["avrs"]
    x["wa"] = ws[0][:, :LANES] + ws[1][:, :LANES]
    x["u0"] = jnp.where(first, ws[0][:, LANES:], ws[1][:, LANES:])
    wcat = jnp.concatenate([x["wa"], x["u0"]], axis=1).astype(BF16)
    aw0 = _dot(arbs[0], wcat)
    aw1 = _dot(arbs[1], wcat)
    x["rq"] = x["rt"] + jnp.where(first, aw0[:, :LANES], aw1[:, :LANES])
    x["y0"] = jnp.where(first, aw0[:, LANES:] + avrs[0], aw1[:, LANES:] + avrs[1])


def _wkv(r, lw, k, v, a, vec8, s0, chunk, tb, pairs):
    b, l, d = r.shape
    width = pairs * LANES
    heads = width // RWKV_HEAD
    row = pl.BlockSpec((None, tb, width), lambda bi, p, t: (bi, t, p))
    st = pl.BlockSpec((None, heads, RWKV_HEAD, RWKV_HEAD), lambda bi, p, t: (bi, p, 0, 0))
    return pl.pallas_call(
        functools.partial(_wkv_body, chunk=chunk),
        grid=(b, d // width, l // tb),
        in_specs=[row] * 5 + [pl.BlockSpec((SUBLANES, width), lambda bi, p, t: (0, p)), st],
        out_specs=[row, st],
        out_shape=[jax.ShapeDtypeStruct((b, l, d), F32),
                   jax.ShapeDtypeStruct((b, RWKV_HEADS, RWKV_HEAD, RWKV_HEAD), F32)],
        scratch_shapes=[pltpu.VMEM((pairs, LANES, LANES), F32)],
        compiler_params=_cparams(("arbitrary", "arbitrary", "arbitrary")),
        name="wkv7_chunked",
    )(r, lw, k, v, a, vec8, s0)


def _pad_rows(a, rows):
    return jnp.pad(a, ((0, rows - a.shape[0]), (0, 0)))


def _mods(m_rows):
    return [t[:, None, :] for t in jnp.split(m_rows, 6, axis=-1)]


def _layer0_weights(w_in, conv_w, a_log, dt_bias, fox_fbias, w_out, ff_w1, ff_w2):
    c = (0, 1536, 2048, 2052, 2056, 2568, 3080, 3592, 3596)
    seg = lambda i: w_in[:, c[i]:c[i + 1]]
    small = jnp.concatenate([seg(2), seg(3), seg(7)], axis=1)
    small = jnp.pad(small, ((0, 0), (0, LANES - small.shape[1])))
    w_all = jnp.concatenate([seg(0), seg(1), seg(4), seg(5), seg(6), small], axis=1).astype(BF16)
    aux = jnp.zeros((SUBLANES, LANES), F32)
    aux = aux.at[0, 4:8].set(a_log).at[1, 4:8].set(dt_bias).at[2, 8:12].set(fox_fbias)
    return dict(w_all=w_all, aux=aux, conv_w8=_pad_rows(conv_w, SUBLANES),
                w_out=w_out.astype(BF16), ff_w1=ff_w1.astype(BF16), ff_w2=ff_w2.astype(BF16))


def _hybrid_layer(x, mods, conv_buf, s0, fox_cache, wts, norm_mix, norm_ff, gdn_onorm, fox_qnorm, fox_knorm,
                  tm, chunk, tb, blk):
    b, l, d = x.shape
    sh1, sc1, gt1, sh2, sc2, gt2 = mods
    row = lambda g: g.reshape(1, -1)
    qkv_raw, z, small, q_b, k_f, v_f, k_b, v_b, k_a, cref = _inproj0(
        x, sh1, sc1, row(norm_mix), wts["w_all"], row(fox_qnorm), row(fox_knorm), wts["aux"], tm)
    conv_new = qkv_raw[:, l - (GDN_CONV - 1):, :]
    logf = small[:, :, 8:12]
    conv_init8 = jnp.pad(conv_buf, ((0, 0), (SUBLANES - (GDN_CONV - 1), 0), (0, 0)))
    o_a, s_new = _gdn(qkv_raw, z, small, conv_init8, s0, wts["conv_w8"], row(gdn_onorm), chunk, tb)

    if fox_cache is None:
        assert blk == tm and b == 1
        o_b = _fox_prompt(q_b[0], k_b[0], k_a[0], v_b[0], jnp.transpose(cref[0, :, 0, 8:8 + FOX_HEADS]), blk)[None]
    else:
        k_cache, v_cache, logf_cache = fox_cache
        p = k_cache.shape[1]
        total = p + l
        padded = -(-total // (SUBLANES * LANES)) * (SUBLANES * LANES)
        lf = jnp.concatenate([jnp.transpose(logf_cache, (0, 2, 1)), jnp.transpose(logf, (0, 2, 1))], axis=2)
        lf = jnp.pad(lf, ((0, 0), (0, 0), (padded - total, 0)))
        cs = _cumsum_rows(lf.reshape(b * FOX_HEADS, padded // LANES, LANES)).reshape(b, FOX_HEADS, padded)
        c_cache = cs[:, :, None, padded - total:padded - l]
        c_new = cs[:, :, padded - l:]
        o_b = _fox_step(q_b, k_f, v_f, k_cache.reshape(b, p, FOX_W), v_cache.reshape(b, p, FOX_W),
                        c_cache, c_new[:, :, None, :], c_new[:, :, :, None])

    x = _mix_mlp(_mix0_mlp_body, "l0_out_mlp", x, [o_a, o_b], [gt1, sh2, sc2, gt2], row(norm_ff),
                 wts["w_out"], wts["ff_w1"], wts["ff_w2"], tm)
    return (x, conv_new, s_new, k_f.reshape(b, l, FOX_HEADS, FOX_DH), v_f.reshape(b, l, FOX_HEADS, FOX_DH), logf)


def _layer1_weights(mu, w_r, w_k, w_v, w0, w1, w2, a0, a1, a2, g1, g2, k_k, k_a, r_k, ln_w, ln_b, w_o, ff_w1, ff_w2):
    padc = lambda w: jnp.pad(w, ((0, 0), (0, -w.shape[1] % LANES))).astype(BF16)
    padr = lambda w: jnp.pad(w, ((0, -w.shape[0] % LANES), (0, 0))).astype(BF16)
    proj = [w_r.astype(BF16), w_k.astype(BF16), w_v.astype(BF16),
            padc(w1), padr(w2), padc(a1), padr(a2), padc(g1), padr(g2)]
    return dict(mu8=_pad_rows(mu, SUBLANES), pvec=_pad_rows(jnp.stack([w0, a0]), SUBLANES),
                svec=_pad_rows(jnp.stack([k_k, k_a, r_k.reshape(-1), ln_w, ln_b]), SUBLANES),
                proj=proj, w_o=w_o.astype(BF16), ff_w1=ff_w1.astype(BF16), ff_w2=ff_w2.astype(BF16))


def _rwkv_layer(x, mods, shift_prev, s0, wts, norm_mix, norm_ff, tm, chunk, tb):
    sh1, sc1, gt1, sh2, sc2, gt2 = mods
    row = lambda g: g.reshape(1, -1)
    r, lw, k, v, a, gate, shift_new = _rwkv_proj(
        x, shift_prev[:, None, :], sh1, sc1, row(norm_mix), wts["mu8"], wts["pvec"], wts["proj"], tm)
    y, s_new = _wkv(r, lw, k, v, a, wts["svec"], s0, chunk, tb, WKV_PAIRS)
    x = _mix_mlp(_mix1_mlp_body, "l1_out_mlp", x, [y, gate], [gt1, sh2, sc2, gt2], row(norm_ff),
                 wts["w_o"], wts["ff_w1"], wts["ff_w2"], tm)
    return x, shift_new[:, 0, :], s_new


def kernel(x_prompt, x_sample, c_prompt, c_sample, cache_l0_conv, state_l0_delta, cache_l0_fox_k, cache_l0_fox_v, cache_l0_fox_logf, state_l1_shift, state_l1_wkv, l0_ada_w, l0_ada_b, l0_norm_mix, l0_norm_ff, l0_w_in, l0_conv_w, l0_a_log, l0_dt_bias, l0_gdn_onorm, l0_fox_qnorm, l0_fox_knorm, l0_fox_fbias, l0_w_out, l0_ff_w1, l0_ff_w2, l1_ada_w, l1_ada_b, l1_norm_mix, l1_norm_ff, l1_mu, l1_w_r, l1_w_k, l1_w_v, l1_w0, l1_w1, l1_w2, l1_a0, l1_a1, l1_a2, l1_g1, l1_g2, l1_k_k, l1_k_a, l1_r_k, l1_ln_w, l1_ln_b, l1_w_o, l1_ff_w1, l1_ff_w2):
    bp, lp, d = x_prompt.shape
    bs, ls, _ = x_sample.shape
    c_all = _pad_rows(jnp.concatenate([c_prompt, c_sample], axis=0), 2 * SUBLANES)
    m0 = _ada(c_all, l0_ada_w, l0_ada_b)
    m1 = _ada(c_all, l1_ada_w, l1_ada_b)
    mods0_p, mods0_s = _mods(m0[:bp]), _mods(m0[bp:bp + bs])
    mods1_p, mods1_s = _mods(m1[:bp]), _mods(m1[bp:bp + bs])

    w0 = _layer0_weights(l0_w_in, l0_conv_w, l0_a_log, l0_dt_bias, l0_fox_fbias, l0_w_out, l0_ff_w1, l0_ff_w2)
    w1 = _layer1_weights(l1_mu, l1_w_r, l1_w_k, l1_w_v, l1_w0, l1_w1, l1_w2, l1_a0, l1_a1, l1_a2, l1_g1, l1_g2,
                         l1_k_k, l1_k_a, l1_r_k, l1_ln_w, l1_ln_b, l1_w_o, l1_ff_w1, l1_ff_w2)
    norms0 = (l0_norm_mix, l0_norm_ff, l0_gdn_onorm, l0_fox_qnorm, l0_fox_knorm)

    tm_p = min(512, lp)
    y_p, conv_p, delta_p, fox_k_p, fox_v_p, fox_logf_p = _hybrid_layer(
        x_prompt, mods0_p, jnp.zeros((bp, GDN_CONV - 1, GDN_QKV), F32),
        jnp.zeros((bp, GDN_HEADS, GDN_DK, GDN_DV), F32), None, w0, *norms0,
        tm=tm_p, chunk=min(GDN_CHUNK, lp), tb=min(256, lp), blk=min(512, lp))
    y_s, conv_s, delta_s, fox_k_s, fox_v_s, fox_logf_s = _hybrid_layer(
        x_sample, mods0_s, cache_l0_conv, state_l0_delta,
        (cache_l0_fox_k, cache_l0_fox_v, cache_l0_fox_logf), w0, *norms0,
        tm=ls, chunk=min(GDN_CHUNK, ls), tb=ls, blk=None)

    y_p, shift_p, wkv_p = _rwkv_layer(
        y_p, mods1_p, jnp.zeros((bp, d), F32), jnp.zeros((bp, RWKV_HEADS, RWKV_HEAD, RWKV_HEAD), F32),
        w1, l1_norm_mix, l1_norm_ff, tm=tm_p, chunk=min(RWKV_CHUNK, lp), tb=min(256, lp))
    y_s, shift_s, wkv_s = _rwkv_layer(
        y_s, mods1_s, state_l1_shift, state_l1_wkv, w1, l1_norm_mix, l1_norm_ff,
        tm=ls, chunk=min(RWKV_CHUNK, ls), tb=ls)

    return (y_p, y_s, conv_p, conv_s, delta_p, delta_s, fox_k_p, fox_k_s,
            fox_v_p, fox_v_s, fox_logf_p, fox_logf_s, shift_p, shift_s, wkv_p, wkv_s)
```

```python
import functools
import math

import jax
import jax.numpy as jnp
from jax import lax
from jax.experimental import pallas as pl
from jax.experimental.pallas import tpu as pltpu

F32 = jnp.float32
BF16 = jnp.bfloat16

D_MODEL = 1024
D_FF = 4 * D_MODEL
NORM_EPS = 1e-6
LOG2E = 1.4426950408889634

GDN_HEADS = 4
GDN_DK = 128
GDN_DV = 128
GDN_CONV = 4
GDN_QKV = GDN_HEADS * (2 * GDN_DK + GDN_DV)
GDN_CHUNK = 64
SOLVE_HI_LEVELS = 3

FOX_HEADS = 4
FOX_DH = 128
FOX_W = FOX_HEADS * FOX_DH
FOX_SPLIT = 2

RWKV_HEAD = 64
RWKV_HEADS = D_MODEL // RWKV_HEAD
RWKV_GN_EPS = 1e-5 * RWKV_HEAD
RWKV_CHUNK = 64
WKV_PAIRS = 2
WKV_SUB = 256

LANES = 128
SUBLANES = 8
VMEM_LIMIT = 56 * 1024 * 1024

NN = (((1,), (0,)), ((), ()))
NT = (((1,), (1,)), ((), ()))
TN = (((0,), (0,)), ((), ()))


def _dot(a, b, dims=NN):
    return lax.dot_general(a, b, dims, preferred_element_type=F32)


def _mm(a, b, dims=NN):
    return _dot(a.astype(BF16), b.astype(BF16), dims)


def _split2(x):
    hi = x.astype(BF16)
    lo = (x - hi.astype(F32)).astype(BF16)
    return hi, lo


def _split3(x):
    hi = x.astype(BF16)
    r = x - hi.astype(F32)
    mid = r.astype(BF16)
    lo = (r - mid.astype(F32)).astype(BF16)
    return hi, mid, lo


def _mm3(a, b, dims=NN):
    ah, al = _split2(a)
    bh, bl = _split2(b)
    return _dot(ah, bh, dims) + (_dot(ah, bl, dims) + _dot(al, bh, dims))


def _mask_mm(mask01, x):
    x1, x2, x3 = _split3(x)
    return _dot(mask01, x1) + (_dot(mask01, x2) + _dot(mask01, x3))


def _sigmoid(x):
    return 1.0 / (1.0 + jnp.exp(-x))


def _softplus(x):
    return jnp.maximum(x, 0.0) + jnp.log(1.0 + jnp.exp(-jnp.abs(x)))


def _silu(x):
    return x * _sigmoid(x)


def _rms_mod(x, g, sc, sh):
    ms = jnp.mean(x * x, axis=-1, keepdims=True)
    return (x * lax.rsqrt(ms + NORM_EPS) * g) * (1.0 + sc) + sh


def _tri_masks(c):
    row = lax.broadcasted_iota(jnp.int32, (c, c), 0)
    col = lax.broadcasted_iota(jnp.int32, (c, c), 1)
    return row >= col, row > col, row == col


def _cparams(sem):
    return pltpu.CompilerParams(dimension_semantics=sem, vmem_limit_bytes=VMEM_LIMIT)


def _const_spec(shape):
    nd = len(shape)
    return pl.BlockSpec(shape, lambda *_: (0,) * nd, pipeline_mode=pl.Buffered(1))


def _ada_body(c_ref, w_ref, b_ref, o_ref):
    o_ref[...] = _mm3(_silu(c_ref[...]), w_ref[...]) + b_ref[...]


def _ada(c_all, w, b):
    rows, d = c_all.shape
    n = w.shape[1]
    tn = 512
    return pl.pallas_call(
        _ada_body,
        grid=(n // tn,),
        in_specs=[pl.BlockSpec((rows, d), lambda j: (0, 0)),
                  pl.BlockSpec((d, tn), lambda j: (0, j)),
                  pl.BlockSpec((1, tn), lambda j: (0, j))],
        out_specs=pl.BlockSpec((rows, tn), lambda j: (0, j)),
        out_shape=jax.ShapeDtypeStruct((rows, n), F32),
        compiler_params=_cparams(("arbitrary",)),
        name="ada_ln",
    )(c_all, w, b.reshape(1, n))


_IN0_QKV = (0, GDN_QKV)
_IN0_Z = (GDN_QKV, GDN_QKV + 512)
_IN0_Q = (2048, 2560)
_IN0_K = (2560, 3072)
_IN0_V = (3072, 3584)
_IN0_S = (3584, 3712)
_IN0_COLS = 3712


def _inproj0_body(x_ref, sh_ref, sc_ref, g_ref, w_ref, qn_ref, kn_ref, aux_ref,
                  qkv_ref, z_ref, small_ref, qb_ref, kf_ref, vf_ref, kb_ref, vb_ref, ka_ref, cref_ref, carry):
    h = _rms_mod(x_ref[...], g_ref[...], sc_ref[...], sh_ref[...]).astype(BF16)
    qkv_ref[...] = _dot(h, w_ref[:, _IN0_QKV[0]:_IN0_QKV[1]])
    z_ref[...] = _dot(h, w_ref[:, _IN0_Z[0]:_IN0_Z[1]])
    q = _dot(h, w_ref[:, _IN0_Q[0]:_IN0_Q[1]])
    k = _dot(h, w_ref[:, _IN0_K[0]:_IN0_K[1]])
    v = _dot(h, w_ref[:, _IN0_V[0]:_IN0_V[1]])
    raw = _dot(h, w_ref[:, _IN0_S[0]:_IN0_S[1]])
    qn = qn_ref[...]
    kn = kn_ref[...]
    for hd in range(FOX_HEADS):
        sl = slice(hd * FOX_DH, (hd + 1) * FOX_DH)
        qh = q[:, sl]
        kh = k[:, sl]
        qh = qh * lax.rsqrt(jnp.mean(qh * qh, axis=-1, keepdims=True) + NORM_EPS) * qn * (FOX_DH ** -0.5 * LOG2E)
        kh = kh * lax.rsqrt(jnp.mean(kh * kh, axis=-1, keepdims=True) + NORM_EPS) * kn
        qb_ref[:, sl] = qh.astype(BF16)
        kf_ref[:, sl] = kh
        kb_ref[:, sl] = kh.astype(BF16)
    vf_ref[...] = v
    vb_ref[...] = v.astype(BF16)
    aux = aux_ref[...]
    a_log, dt_bias, f_bias = aux[0:1], aux[1:2], aux[2:3]
    lane = lax.broadcasted_iota(jnp.int32, raw.shape, 1)
    beta = _sigmoid(raw)
    gdec = -jnp.exp(a_log) * _softplus(raw + dt_bias)
    logf = -_softplus(-(raw + f_bias))
    small_ref[...] = jnp.where(lane < 4, beta, jnp.where(lane < 8, gdec, jnp.where(lane < 12, logf, 0.0)))

    @pl.when(pl.program_id(1) == 0)
    def _():
        carry[...] = jnp.zeros_like(carry)

    tm = raw.shape[0]
    tri, _, _ = _tri_masks(tm)
    within = _mask_mm(tri.astype(BF16), logf * LOG2E)
    cref_ref[...] = carry[...] + within[0:1, :]
    carry[...] = carry[...] + within[tm - 1:tm, :]
    bias = within[0:1, :] - within
    lane_h = lax.broadcasted_iota(jnp.int32, (tm, FOX_DH), 1)
    for hd in range(FOX_HEADS):
        col = bias[:, 8 + hd:9 + hd]
        b1 = col.astype(BF16).astype(F32)
        b2 = (col - b1).astype(BF16).astype(F32)
        b3 = col - b1 - b2
        cols = jnp.where(lane_h == 0, b1, jnp.where(lane_h == 1, b2, jnp.where(lane_h == 2, b3, 0.0)))
        ka_ref[:, hd * FOX_DH:(hd + 1) * FOX_DH] = cols.astype(BF16)


def _inproj0(x, sh, sc, g, w_all, qn, kn, aux, tm):
    b, l, d = x.shape
    row = lambda width: pl.BlockSpec((None, tm, width), lambda bi, i: (bi, i, 0))
    mod = pl.BlockSpec((None, 1, d), lambda bi, i: (bi, 0, 0))
    out = lambda width, dt: jax.ShapeDtypeStruct((b, l, width), dt)
    return pl.pallas_call(
        _inproj0_body,
        grid=(b, l // tm),
        in_specs=[row(d), mod, mod, _const_spec((1, d)), _const_spec((d, _IN0_COLS)),
                  _const_spec((1, FOX_DH)), _const_spec((1, FOX_DH)), _const_spec((SUBLANES, LANES))],
        out_specs=[row(GDN_QKV), row(512), row(LANES), row(FOX_W), row(FOX_W), row(FOX_W), row(FOX_W), row(FOX_W),
                   row(FOX_W), pl.BlockSpec((None, None, 1, LANES), lambda bi, i: (bi, i, 0, 0))],
        out_shape=[out(GDN_QKV, F32), out(512, F32), out(LANES, F32), out(FOX_W, BF16),
                   out(FOX_W, F32), out(FOX_W, F32), out(FOX_W, BF16), out(FOX_W, BF16), out(FOX_W, BF16),
                   jax.ShapeDtypeStruct((b, l // tm, 1, LANES), F32)],
        scratch_shapes=[pltpu.VMEM((1, LANES), F32)],
        compiler_params=_cparams(("arbitrary", "arbitrary")),
        name="l0_in_proj",
    )(x, sh, sc, g, w_all, qn, kn, aux)


def _chunk_masks(tb, chunk):
    shift = int(math.log2(chunk))
    row = lax.broadcasted_iota(jnp.int32, (tb, tb), 0)
    col = lax.broadcasted_iota(jnp.int32, (tb, tb), 1)
    same = (row >> shift) == (col >> shift)
    return same & (row >= col), same & (row > col), row == col


def _gdn_body(qkv_ref, z_ref, small_ref, cinit_ref, s0_ref, cw_ref, on_ref,
              o_ref, sout_ref, ext, state, *, chunk):
    c = chunk
    tb = qkv_ref.shape[0]
    ci = pl.program_id(1)

    @pl.when(ci == 0)
    def _():
        ext[0:SUBLANES, :] = cinit_ref[...]
        state[...] = s0_ref[...]

    ext[SUBLANES:SUBLANES + tb, :] = qkv_ref[...]
    cw = cw_ref[...]
    base = SUBLANES - (GDN_CONV - 1)
    y = ext[base:base + tb, :] * cw[0:1]
    for i in range(1, GDN_CONV):
        y = y + ext[base + i:base + i + tb, :] * cw[i:i + 1]
    qkv = _silu(y)
    ext[0:SUBLANES, :] = ext[tb:tb + SUBLANES, :]

    tri, strict, eye = _chunk_masks(tb, c)
    sm = small_ref[...]
    gc_all = _mask_mm(tri.astype(BF16), sm)
    onorm = on_ref[...]
    eye_d = _tri_masks(GDN_DK)[2]
    heads = []
    for hd in range(GDN_HEADS):
        q = qkv[:, hd * GDN_DK:(hd + 1) * GDN_DK]
        k = qkv[:, 512 + hd * GDN_DK:512 + (hd + 1) * GDN_DK]
        v = qkv[:, 1024 + hd * GDN_DV:1024 + (hd + 1) * GDN_DV]
        q = q * lax.rsqrt(jnp.sum(q * q, axis=-1, keepdims=True) + NORM_EPS) * GDN_DK ** -0.5
        k = k * lax.rsqrt(jnp.sum(k * k, axis=-1, keepdims=True) + NORM_EPS)
        beta = sm[:, hd:hd + 1]
        gc = gc_all[:, 4 + hd:5 + hd]
        gc_row = jnp.sum(jnp.where(eye, gc, 0.0), axis=0, keepdims=True)
        dec = jnp.where(tri, jnp.exp(jnp.where(tri, gc - gc_row, 0.0)), 0.0)
        kb = k.astype(BF16)
        egc = jnp.exp(gc)
        heads.append(dict(k=k, qe=q * egc,
                          n=-jnp.where(strict, beta * _dot(kb, kb, NT) * dec, 0.0),
                          sol=jnp.concatenate([k * (beta * egc), v * beta], axis=1),
                          qk=_dot(q.astype(BF16), kb, NT) * dec))
    levels = max(1, int(math.ceil(math.log2(c))))
    for lvl in range(levels):
        mm = _mm3 if lvl < SOLVE_HI_LEVELS else _mm
        for x in heads:
            x["sol"] = x["sol"] + mm(x["n"], x["sol"])
        if lvl + 1 < levels:
            for x in heads:
                x["n"] = mm(x["n"], x["n"])
    for x in heads:
        qs = _mm(x["qk"], x["sol"])
        x["oq"] = x["qe"] - qs[:, :GDN_DK]
        x["o0"] = qs[:, GDN_DK:]
    s_hs = [state[hd] for hd in range(GDN_HEADS)]
    for c0 in range(0, tb, c):
        rows = slice(c0, c0 + c)
        for hd, x in enumerate(heads):
            g_last = gc_all[c0 + c - 1:c0 + c, 4 + hd:5 + hd]
            ks = _mm3(x["k"][rows] * jnp.exp(g_last - gc_all[rows, 4 + hd:5 + hd]), x["sol"][rows], TN)
            o = _mm3(x["oq"][rows], s_hs[hd]) + x["o0"][rows]
            s_hs[hd] = _mm3(jnp.where(eye_d, jnp.exp(g_last), 0.0) - ks[:, :GDN_DK], s_hs[hd]) + ks[:, GDN_DK:]
            zz = z_ref[rows, hd * GDN_DV:(hd + 1) * GDN_DV]
            o = o * lax.rsqrt(jnp.mean(o * o, axis=-1, keepdims=True) + NORM_EPS) * onorm
            o_ref[rows, hd * GDN_DV:(hd + 1) * GDN_DV] = (o * _silu(zz)).astype(BF16)
    for hd in range(GDN_HEADS):
        state[hd] = s_hs[hd]

    @pl.when(ci == pl.num_programs(1) - 1)
    def _():
        sout_ref[...] = state[...]


def _gdn(qkv_raw, z, small, conv_init8, s0, conv_w8, onorm, chunk, tb):
    b, l, _ = qkv_raw.shape
    row = lambda width: pl.BlockSpec((None, tb, width), lambda bi, i: (bi, i, 0))
    st = pl.BlockSpec((None, GDN_HEADS, GDN_DK, GDN_DV), lambda bi, i: (bi, 0, 0, 0))
    return pl.pallas_call(
        functools.partial(_gdn_body, chunk=chunk),
        grid=(b, l // tb),
        in_specs=[row(GDN_QKV), row(512), row(LANES),
                  pl.BlockSpec((None, SUBLANES, GDN_QKV), lambda bi, i: (bi, 0, 0)), st,
                  _const_spec((SUBLANES, GDN_QKV)), _const_spec((1, GDN_DV))],
        out_specs=[row(512), st],
        out_shape=[jax.ShapeDtypeStruct((b, l, 512), BF16),
                   jax.ShapeDtypeStruct((b, GDN_HEADS, GDN_DK, GDN_DV), F32)],
        scratch_shapes=[pltpu.VMEM((tb + SUBLANES, GDN_QKV), F32),
                        pltpu.VMEM((GDN_HEADS, GDN_DK, GDN_DV), F32)],
        compiler_params=_cparams(("arbitrary", "arbitrary")),
        name="gdn_mixer",
    )(qkv_raw, z, small, conv_init8, s0, conv_w8, onorm)


def _cumsum_body(x_ref, o_ref):
    x = x_ref[...]
    rows = x.shape[0]
    li = lax.broadcasted_iota(jnp.int32, (LANES, LANES), 0)
    lj = lax.broadcasted_iota(jnp.int32, (LANES, LANES), 1)
    upper = (li <= lj).astype(BF16)
    ones = jnp.ones((LANES, LANES), BF16)
    ri = lax.broadcasted_iota(jnp.int32, (rows, rows), 0)
    rj = lax.broadcasted_iota(jnp.int32, (rows, rows), 1)
    before = (rj < ri).astype(BF16)
    x1, x2, x3 = _split3(x)
    within = _dot(x1, upper) + (_dot(x2, upper) + _dot(x3, upper))
    total = _dot(x1, ones) + (_dot(x2, ones) + _dot(x3, ones))
    o_ref[...] = within + _mask_mm(before, total)


def _cumsum_rows(x):
    g, rows, _ = x.shape
    spec = pl.BlockSpec((None, rows, LANES), lambda i: (i, 0, 0))
    return pl.pallas_call(
        _cumsum_body, grid=(g,), in_specs=[spec], out_specs=spec,
        out_shape=jax.ShapeDtypeStruct(x.shape, F32),
        compiler_params=_cparams(("arbitrary",)),
        name="logf_cumsum",
    )(x)


def _fox_prompt_body(cref_ref, q_ref, k_ref, ka_ref, v_ref, o_ref, s_a, s_b, m_scr, acc_scr, *, blk):
    hd = pl.program_id(0)
    qi = pl.program_id(1)
    sub = blk // FOX_SPLIT
    half = blk // 2
    lane = lax.broadcasted_iota(jnp.int32, (blk, FOX_DH), 1)
    ones = jnp.ones((half, FOX_DH), BF16)
    q = jnp.concatenate([q_ref[...], jnp.where(lane < 3, 1.0, 0.0).astype(BF16)], axis=1)
    c0 = cref_ref[hd, qi]
    row = lax.broadcasted_iota(jnp.int32, (sub, half), 0)
    col = lax.broadcasted_iota(jnp.int32, (sub, half), 1)
    m_scr[...] = jnp.full_like(m_scr, -jnp.inf)
    acc_scr[...] = jnp.zeros_like(acc_scr)

    def scores(h, s_ref):
        rows = pl.ds(pl.multiple_of(h * half, half), half)
        kh = jnp.concatenate([k_ref[rows, :], ka_ref[rows, :]], axis=1)
        for i in range(FOX_SPLIT):
            s_ref[i * sub:(i + 1) * sub, :] = _dot(q[i * sub:(i + 1) * sub], kh, NT)

    def consume(h, s_ref, delta, first_col):
        rows = pl.ds(pl.multiple_of(h * half, half), half)
        vh = jnp.concatenate([v_ref[rows, :], ones], axis=1)
        for i in range(FOX_SPLIT):
            rs = slice(i * sub, (i + 1) * sub)
            s = s_ref[rs, :]
            if first_col is not None:
                s = jnp.where(col + first_col <= row + i * sub, s, -jnp.inf)
            m = m_scr[rs, :]
            m_new = jnp.maximum(m, jnp.max(s, axis=-1, keepdims=True) + delta)
            shift = m_new - delta
            p = jnp.concatenate([jnp.exp2(s[:, g * LANES:(g + 1) * LANES] - shift) for g in range(half // LANES)],
                                axis=1)
            pv = _dot(p.astype(BF16), vh)
            alpha = jnp.exp2(m - m_new)
            acc = acc_scr[rs, :]
            acc_scr[rs, :] = jnp.concatenate(
                [alpha * acc[:, g * LANES:(g + 1) * LANES] + pv[:, g * LANES:(g + 1) * LANES] for g in range(2)],
                axis=1)
            m_scr[rs, :] = m_new

    def past_block(j):
        delta = c0 - cref_ref[hd, j]
        scores(2 * j + 1, s_b)
        consume(2 * j, s_a, delta, None)
        scores(2 * j + 2, s_a)
        consume(2 * j + 1, s_b, delta, None)

    def two_past_blocks(jj, carry):
        past_block(2 * jj)
        past_block(2 * jj + 1)
        return carry

    scores(0, s_a)
    lax.fori_loop(0, qi // 2, two_past_blocks, 0)

    @pl.when(qi % 2 == 1)
    def _():
        past_block(qi - 1)

    scores(2 * qi + 1, s_b)
    consume(2 * qi, s_a, 0.0, 0)
    consume(2 * qi + 1, s_b, 0.0, half)
    acc = acc_scr[...]
    o_ref[...] = (acc[:, :FOX_DH] / acc[:, FOX_DH:FOX_DH + 1]).astype(BF16)


def _fox_prompt(q, k, ka, v, cref, blk):
    l = q.shape[0]
    nb = l // blk
    res = pl.BlockSpec((l, FOX_DH), lambda h, i: (0, h))
    return pl.pallas_call(
        functools.partial(_fox_prompt_body, blk=blk),
        grid=(FOX_HEADS, nb),
        in_specs=[pl.BlockSpec(memory_space=pltpu.SMEM),
                  pl.BlockSpec((blk, FOX_DH), lambda h, i: (i, h)), res, res, res],
        out_specs=pl.BlockSpec((blk, FOX_DH), lambda h, i: (i, h)),
        out_shape=jax.ShapeDtypeStruct((l, FOX_W), BF16),
        scratch_shapes=[pltpu.VMEM((blk, blk // 2), F32), pltpu.VMEM((blk, blk // 2), F32),
                        pltpu.VMEM((blk, LANES), F32), pltpu.VMEM((blk, 2 * FOX_DH), F32)],
        compiler_params=_cparams(("arbitrary", "arbitrary")),
        name="fox_prompt_attn",
    )(cref, q, k, ka, v)


def _fox_step_body(q_ref, kn_ref, vn_ref, kc_ref, vc_ref, cc_ref, cnr_ref, cnc_ref, o_ref):
    q = q_ref[...]
    n = q.shape[0]
    c_i = cnc_ref[...]
    s_c = _mm(q, kc_ref[...], NT) + (c_i - cc_ref[...]) * LOG2E
    s_n = _mm(q, kn_ref[...], NT) + (c_i - cnr_ref[...]) * LOG2E
    tri, _, _ = _tri_masks(n)
    s_n = jnp.where(tri, s_n, -jnp.inf)
    m = jnp.maximum(jnp.max(s_c, axis=-1, keepdims=True), jnp.max(s_n, axis=-1, keepdims=True))
    p_c = jnp.exp2(s_c - m)
    p_n = jnp.exp2(s_n - m)
    l = jnp.sum(p_c, axis=-1, keepdims=True) + jnp.sum(p_n, axis=-1, keepdims=True)
    o = _mm(p_c, vc_ref[...]) + _mm(p_n, vn_ref[...])
    o_ref[...] = (o / l).astype(BF16)


def _fox_step(q, k_new, v_new, k_cache, v_cache, c_cache, c_new_row, c_new_col):
    b, n, _ = q.shape
    p = k_cache.shape[1]
    new = pl.BlockSpec((None, n, FOX_DH), lambda bi, h: (bi, 0, h))
    cache = pl.BlockSpec((None, p, FOX_DH), lambda bi, h: (bi, 0, h))
    return pl.pallas_call(
        _fox_step_body,
        grid=(b, FOX_HEADS),
        in_specs=[new, new, new, cache, cache,
                  pl.BlockSpec((None, None, 1, p), lambda bi, h: (bi, h, 0, 0)),
                  pl.BlockSpec((None, None, 1, n), lambda bi, h: (bi, h, 0, 0)),
                  pl.BlockSpec((None, None, n, 1), lambda bi, h: (bi, h, 0, 0))],
        out_specs=new,
        out_shape=jax.ShapeDtypeStruct((b, n, FOX_W), BF16),
        compiler_params=_cparams(("arbitrary", "arbitrary")),
        name="fox_step_attn",
    )(q, k_new, v_new, k_cache, v_cache, c_cache, c_new_row, c_new_col)


def _resid_mlp(x, mix, gt1, sh2, sc2, gt2, g2, w1_ref, w2_ref):
    x1 = x + gt1 * mix
    h2 = _rms_mod(x1, g2, sc2, sh2).astype(BF16)
    acc = jnp.zeros_like(x1)
    step = D_MODEL
    for c0 in range(0, D_FF, step):
        hid = jnp.maximum(_dot(h2, w1_ref[:, c0:c0 + step]), 0.0)
        acc = acc + _dot((hid * hid).astype(BF16), w2_ref[c0:c0 + step, :])
    return x1 + gt2 * acc


def _mix0_mlp_body(x_ref, oa_ref, ob_ref, gt1, sh2, sc2, gt2, g2, wo_ref, w1_ref, w2_ref, out_ref):
    half = GDN_HEADS * GDN_DV
    mix = _dot(oa_ref[...], wo_ref[0:half, :]) + _dot(ob_ref[...], wo_ref[half:, :])
    out_ref[...] = _resid_mlp(x_ref[...], mix, gt1[...], sh2[...], sc2[...], gt2[...], g2[...], w1_ref, w2_ref)


def _mix1_mlp_body(x_ref, y_ref, gate_ref, gt1, sh2, sc2, gt2, g2, wo_ref, w1_ref, w2_ref, out_ref):
    mix = _mm(y_ref[...] * gate_ref[...], wo_ref[...])
    out_ref[...] = _resid_mlp(x_ref[...], mix, gt1[...], sh2[...], sc2[...], gt2[...], g2[...], w1_ref, w2_ref)


def _mix_mlp(body, name, x, mix_in, mods, g2, wo, w1, w2, tm):
    b, l, d = x.shape
    row = lambda width: pl.BlockSpec((None, tm, width), lambda bi, i: (bi, i, 0))
    mod = pl.BlockSpec((None, 1, d), lambda bi, i: (bi, 0, 0))
    return pl.pallas_call(
        body,
        grid=(b, l // tm),
        in_specs=[row(d)] + [row(a.shape[-1]) for a in mix_in] + [mod] * 4
                 + [_const_spec((1, d)), _const_spec(wo.shape), _const_spec(w1.shape), _const_spec(w2.shape)],
        out_specs=row(d),
        out_shape=jax.ShapeDtypeStruct((b, l, d), F32),
        compiler_params=_cparams(("arbitrary", "arbitrary")),
        name=name,
    )(x, *mix_in, *mods, g2, wo, w1, w2)


def _rwkv_proj_body(x_ref, shp_ref, sh_ref, sc_ref, g_ref, mu_ref, vec_ref,
                    wr_ref, wk_ref, wv_ref, w1_ref, w2_ref, a1_ref, a2_ref, g1_ref, g2_ref,
                    r_ref, lw_ref, k_ref, v_ref, a_ref, gate_ref, shift_ref, hext):
    tm = x_ref.shape[0]
    i = pl.program_id(1)
    h = _rms_mod(x_ref[...], g_ref[...], sc_ref[...], sh_ref[...])

    @pl.when(i == 0)
    def _():
        hext[SUBLANES - 1:SUBLANES, :] = shp_ref[...]

    hext[SUBLANES:SUBLANES + tm, :] = h
    h_prev = hext[SUBLANES - 1:SUBLANES - 1 + tm, :]
    hext[SUBLANES - 1:SUBLANES, :] = h[tm - 1:tm, :]
    shift_ref[...] = h[tm - 1:tm, :]

    xx = h_prev - h
    mu = mu_ref[...]
    vec = vec_ref[...]
    w0, a0 = vec[0:1], vec[1:2]
    mixed = lambda j: (h + xx * mu[j:j + 1]).astype(BF16)
    r_ref[...] = _dot(mixed(0), wr_ref[...])
    wl = w0 + _mm(jnp.tanh(_dot(mixed(1), w1_ref[...])), w2_ref[...])
    lw_ref[...] = -jnp.exp(-_softplus(-wl) - 0.5)
    k_ref[...] = _dot(mixed(2), wk_ref[...])
    v_ref[...] = _dot(mixed(3), wv_ref[...])
    a_ref[...] = _sigmoid(a0 + _mm(_dot(mixed(4), a1_ref[...]), a2_ref[...]))
    gate_ref[...] = _mm(_sigmoid(_dot(mixed(5), g1_ref[...])), g2_ref[...])


def _rwkv_proj(x, shift_prev, sh, sc, g, mu8, vec8, weights, tm):
    b, l, d = x.shape
    row = pl.BlockSpec((None, tm, d), lambda bi, i: (bi, i, 0))
    mod = pl.BlockSpec((None, 1, d), lambda bi, i: (bi, 0, 0))
    big = jax.ShapeDtypeStruct((b, l, d), F32)
    return pl.pallas_call(
        _rwkv_proj_body,
        grid=(b, l // tm),
        in_specs=[row, mod, mod, mod, _const_spec((1, d)), _const_spec((SUBLANES, d)), _const_spec((SUBLANES, d))]
                 + [_const_spec(w.shape) for w in weights],
        out_specs=[row] * 6 + [mod],
        out_shape=[big] * 6 + [jax.ShapeDtypeStruct((b, 1, d), F32)],
        scratch_shapes=[pltpu.VMEM((tm + SUBLANES, d), F32)],
        compiler_params=_cparams(("arbitrary", "arbitrary")),
        name="rwkv_proj",
    )(x, shift_prev, sh, sc, g, mu8, vec8, *weights)


def _pair_sum(x, first):
    lo = jnp.sum(jnp.where(first, x, 0.0), axis=-1, keepdims=True)
    hi = jnp.sum(jnp.where(first, 0.0, x), axis=-1, keepdims=True)
    return jnp.where(first, lo, hi)


def _wkv_body(r_ref, lw_ref, k_ref, v_ref, a_ref, vec_ref, s0_ref, y_ref, sout_ref, state, *, chunk):
    c = chunk
    tb = r_ref.shape[0]
    n = RWKV_HEAD
    t = pl.program_id(2)
    pairs = r_ref.shape[1] // LANES

    @pl.when(t == 0)
    def _():
        state[...] = jnp.zeros_like(state)
        for p in range(pairs):
            state[p, 0:n, 0:n] = s0_ref[2 * p]
            state[p, n:, n:] = s0_ref[2 * p + 1]

    ts = min(tb, WKV_SUB)
    tri, strict, eye = _chunk_masks(ts, c)
    first = lax.broadcasted_iota(jnp.int32, (ts, LANES), 1) < n
    hr = lax.broadcasted_iota(jnp.int32, (LANES, LANES), 0)
    hc = lax.broadcasted_iota(jnp.int32, (LANES, LANES), 1)
    same_head = (hr < n) == (hc < n)
    eye_l = jnp.where(hr == hc, 1.0, 0.0)

    fc = lax.broadcasted_iota(jnp.int32, (c, LANES), 1) < n
    levels = max(1, int(math.ceil(math.log2(c))))
    lanes = [slice(p * LANES, (p + 1) * LANES) for p in range(pairs)]
    subs = [slice(t0, t0 + ts) for t0 in range(0, tb, ts)]

    ctx = [_wkv_scores(r_ref[rs, ls], lw_ref[rs, ls], k_ref[rs, ls], v_ref[rs, ls], a_ref[rs, ls], vec_ref[:, ls],
                       c, tri, strict, eye, first) for rs in subs for ls in lanes]
    for _ in range(1, levels):
        for x in ctx:
            x["nks"] = [_mm(nk, nk) for nk in x["nks"]]
        for x in ctx:
            x["t_invs"] = [t_inv + _mm(nk, t_inv) for nk, t_inv in zip(x["nks"], x["t_invs"])]
    for x in ctx:
        x["ws"] = [_mm(t_inv, rhs) for t_inv, rhs in zip(x["t_invs"], x["rhs"])]
    for x in ctx:
        _wkv_state_free_terms(x, first)

    sps = [state[p] for p in range(pairs)]
    for c0 in range(0, tb, c):
        sub, off = divmod(c0, ts)
        rows = slice(off, off + c)
        for p in range(pairs):
            x = ctx[sub * pairs + p]
            gam = jnp.exp(x["cl"][off + c - 1:off + c, :])
            gp = jnp.where(same_head, eye_l + _mm(x["wa"][rows], x["bt"][rows], TN), 0.0) * gam
            hp = jnp.where(same_head,
                           _dot(jnp.concatenate([x["u0"][rows], x["v"][rows]], axis=0).astype(BF16),
                                jnp.concatenate([x["bt"][rows], x["kt"][rows]], axis=0).astype(BF16), TN), 0.0) * gam
            y = _mm(x["rq"][rows], sps[p], NT) + x["y0"][rows]
            sps[p] = _mm(sps[p], gp) + hp
            yc = y - _pair_sum(y, fc) * (1.0 / n)
            var = _pair_sum(yc * yc, fc) * (1.0 / n)
            y_ref[c0:c0 + c, lanes[p]] = (yc * lax.rsqrt(var + RWKV_GN_EPS) * x["ln_w"] + x["ln_b"]
                                          + x["bonus"][rows])
    for p in range(pairs):
        state[p] = sps[p]

    @pl.when(t == pl.num_programs(2) - 1)
    def _():
        for p in range(pairs):
            sout_ref[2 * p] = state[p, 0:n, 0:n]
            sout_ref[2 * p + 1] = state[p, n:, n:]


def _wkv_scores(r, lw, k, v, a, vec, c, tri, strict, eye, first):
    tb = r.shape[0]
    k_k, k_a, r_k, ln_w, ln_b = vec[0:1], vec[1:2], vec[2:3], vec[3:4], vec[4:5]
    kq = k * k_k
    kkn = kq * lax.rsqrt(_pair_sum(kq * kq, first) + NORM_EPS)
    kp = k * (1.0 + (a - 1.0) * k_a)
    cl = _mask_mm(tri.astype(BF16), lw)
    e_neg = jnp.exp(-cl)
    at = -kkn * jnp.exp(cl - lw)
    rt = r * jnp.exp(cl)
    bt = (kkn * a) * e_neg
    kt = kp * e_neg
    kcat = jnp.concatenate([bt, kt], axis=0).astype(BF16)
    vb = v.astype(BF16)
    x = dict(cl=cl, rt=rt, bt=bt, kt=kt, v=v, ln_w=ln_w, ln_b=ln_b,
             bonus=_pair_sum(r * kp * r_k, first) * v, arbs=[], avrs=[], nks=[], t_invs=[], rhs=[])
    for head_lanes in (first, jnp.logical_not(first)):
        at_e = jnp.where(head_lanes, at, 0.0)
        ma = _dot(at_e.astype(BF16), kcat, NT)
        mr = _dot(jnp.where(head_lanes, rt, 0.0).astype(BF16), kcat, NT)
        nk = jnp.where(strict, ma[:, :tb], 0.0)
        av_a = _dot(jnp.where(strict, ma[:, tb:], 0.0).astype(BF16), vb)
        x["avrs"].append(_dot(jnp.where(tri, mr[:, tb:], 0.0).astype(BF16), vb))
        x["arbs"].append(jnp.where(tri, mr[:, :tb], 0.0).astype(BF16))
        x["nks"].append(nk)
        x["t_invs"].append(jnp.where(eye, 1.0, nk))
        x["rhs"].append(jnp.concatenate([at_e, av_a], axis=1))
    return x


def _wkv_state_free_terms(x, first):
    ws, arbs, avrs = x["ws"], x["arbs"], x["avrs"]
    x["wa"] = ws[0][:, :LANES] + ws[1][:, :LANES]
    x["u0"] = jnp.where(first, ws[0][:, LANES:], ws[1][:, LANES:])
    wcat = jnp.concatenate([x["wa"], x["u0"]], axis=1).astype(BF16)
    aw0 = _dot(arbs[0], wcat)
    aw1 = _dot(arbs[1], wcat)
    x["rq"] = x["rt"] + jnp.where(first, aw0[:, :LANES], aw1[:, :LANES])
    x["y0"] = jnp.where(first, aw0[:, LANES:] + avrs[0], aw1[:, LANES:] + avrs[1])


def _wkv(r, lw, k, v, a, vec8, s0, chunk, tb, pairs):
    b, l, d = r.shape
    width = pairs * LANES
    heads = width // RWKV_HEAD
    row = pl.BlockSpec((None, tb, width), lambda bi, p, t: (bi, t, p))
    st = pl.BlockSpec((None, heads, RWKV_HEAD, RWKV_HEAD), lambda bi, p, t: (bi, p, 0, 0))
    return pl.pallas_call(
        functools.partial(_wkv_body, chunk=chunk),
        grid=(b, d // width, l // tb),
        in_specs=[row] * 5 + [pl.BlockSpec((SUBLANES, width), lambda bi, p, t: (0, p)), st],
        out_specs=[row, st],
        out_shape=[jax.ShapeDtypeStruct((b, l, d), F32),
                   jax.ShapeDtypeStruct((b, RWKV_HEADS, RWKV_HEAD, RWKV_HEAD), F32)],
        scratch_shapes=[pltpu.VMEM((pairs, LANES, LANES), F32)],
        compiler_params=_cparams(("arbitrary", "arbitrary", "arbitrary")),
        name="wkv7_chunked",
    )(r, lw, k, v, a, vec8, s0)


def _pad_rows(a, rows):
    return jnp.pad(a, ((0, rows - a.shape[0]), (0, 0)))


def _mods(m_rows):
    return [t[:, None, :] for t in jnp.split(m_rows, 6, axis=-1)]


def _layer0_weights(w_in, conv_w, a_log, dt_bias, fox_fbias, w_out, ff_w1, ff_w2):
    c = (0, 1536, 2048, 2052, 2056, 2568, 3080, 3592, 3596)
    seg = lambda i: w_in[:, c[i]:c[i + 1]]
    small = jnp.concatenate([seg(2), seg(3), seg(7)], axis=1)
    small = jnp.pad(small, ((0, 0), (0, LANES - small.shape[1])))
    w_all = jnp.concatenate([seg(0), seg(1), seg(4), seg(5), seg(6), small], axis=1).astype(BF16)
    aux = jnp.zeros((SUBLANES, LANES), F32)
    aux = aux.at[0, 4:8].set(a_log).at[1, 4:8].set(dt_bias).at[2, 8:12].set(fox_fbias)
    return dict(w_all=w_all, aux=aux, conv_w8=_pad_rows(conv_w, SUBLANES),
                w_out=w_out.astype(BF16), ff_w1=ff_w1.astype(BF16), ff_w2=ff_w2.astype(BF16))


def _hybrid_layer(x, mods, conv_buf, s0, fox_cache, wts, norm_mix, norm_ff, gdn_onorm, fox_qnorm, fox_knorm,
                  tm, chunk, tb, blk):
    b, l, d = x.shape
    sh1, sc1, gt1, sh2, sc2, gt2 = mods
    row = lambda g: g.reshape(1, -1)
    qkv_raw, z, small, q_b, k_f, v_f, k_b, v_b, k_a, cref = _inproj0(
        x, sh1, sc1, row(norm_mix), wts["w_all"], row(fox_qnorm), row(fox_knorm), wts["aux"], tm)
    conv_new = qkv_raw[:, l - (GDN_CONV - 1):, :]
    logf = small[:, :, 8:12]
    conv_init8 = jnp.pad(conv_buf, ((0, 0), (SUBLANES - (GDN_CONV - 1), 0), (0, 0)))
    o_a, s_new = _gdn(qkv_raw, z, small, conv_init8, s0, wts["conv_w8"], row(gdn_onorm), chunk, tb)

    if fox_cache is None:
        assert blk == tm and b == 1
        o_b = _fox_prompt(q_b[0], k_b[0], k_a[0], v_b[0], jnp.transpose(cref[0, :, 0, 8:8 + FOX_HEADS]), blk)[None]
    else:
        k_cache, v_cache, logf_cache = fox_cache
        p = k_cache.shape[1]
        total = p + l
        padded = -(-total // (SUBLANES * LANES)) * (SUBLANES * LANES)
        lf = jnp.concatenate([jnp.transpose(logf_cache, (0, 2, 1)), jnp.transpose(logf, (0, 2, 1))], axis=2)
        lf = jnp.pad(lf, ((0, 0), (0, 0), (padded - total, 0)))
        cs = _cumsum_rows(lf.reshape(b * FOX_HEADS, padded // LANES, LANES)).reshape(b, FOX_HEADS, padded)
        c_cache = cs[:, :, None, padded - total:padded - l]
        c_new = cs[:, :, padded - l:]
        o_b = _fox_step(q_b, k_f, v_f, k_cache.reshape(b, p, FOX_W), v_cache.reshape(b, p, FOX_W),
                        c_cache, c_new[:, :, None, :], c_new[:, :, :, None])

    x = _mix_mlp(_mix0_mlp_body, "l0_out_mlp", x, [o_a, o_b], [gt1, sh2, sc2, gt2], row(norm_ff),
                 wts["w_out"], wts["ff_w1"], wts["ff_w2"], tm)
    return (x, conv_new, s_new, k_f.reshape(b, l, FOX_HEADS, FOX_DH), v_f.reshape(b, l, FOX_HEADS, FOX_DH), logf)


def _layer1_weights(mu, w_r, w_k, w_v, w0, w1, w2, a0, a1, a2, g1, g2, k_k, k_a, r_k, ln_w, ln_b, w_o, ff_w1, ff_w2):
    padc = lambda w: jnp.pad(w, ((0, 0), (0, -w.shape[1] % LANES))).astype(BF16)
    padr = lambda w: jnp.pad(w, ((0, -w.shape[0] % LANES), (0, 0))).astype(BF16)
    proj = [w_r.astype(BF16), w_k.astype(BF16), w_v.astype(BF16),
            padc(w1), padr(w2), padc(a1), padr(a2), padc(g1), padr(g2)]
    return dict(mu8=_pad_rows(mu, SUBLANES), pvec=_pad_rows(jnp.stack([w0, a0]), SUBLANES),
                svec=_pad_rows(jnp.stack([k_k, k_a, r_k.reshape(-1), ln_w, ln_b]), SUBLANES),
                proj=proj, w_o=w_o.astype(BF16), ff_w1=ff_w1.astype(BF16), ff_w2=ff_w2.astype(BF16))


def _rwkv_layer(x, mods, shift_prev, s0, wts, norm_mix, norm_ff, tm, chunk, tb):
    sh1, sc1, gt1, sh2, sc2, gt2 = mods
    row = lambda g: g.reshape(1, -1)
    r, lw, k, v, a, gate, shift_new = _rwkv_proj(
        x, shift_prev[:, None, :], sh1, sc1, row(norm_mix), wts["mu8"], wts["pvec"], wts["proj"], tm)
    y, s_new = _wkv(r, lw, k, v, a, wts["svec"], s0, chunk, tb, WKV_PAIRS)
    x = _mix_mlp(_mix1_mlp_body, "l1_out_mlp", x, [y, gate], [gt1, sh2, sc2, gt2], row(norm_ff),
                 wts["w_o"], wts["ff_w1"], wts["ff_w2"], tm)
    return x, shift_new[:, 0, :], s_new


def kernel(x_prompt, x_sample, c_prompt, c_sample, cache_l0_conv, state_l0_delta, cache_l0_fox_k, cache_l0_fox_v, cache_l0_fox_logf, state_l1_shift, state_l1_wkv, l0_ada_w, l0_ada_b, l0_norm_mix, l0_norm_ff, l0_w_in, l0_conv_w, l0_a_log, l0_dt_bias, l0_gdn_onorm, l0_fox_qnorm, l0_fox_knorm, l0_fox_fbias, l0_w_out, l0_ff_w1, l0_ff_w2, l1_ada_w, l1_ada_b, l1_norm_mix, l1_norm_ff, l1_mu, l1_w_r, l1_w_k, l1_w_v, l1_w0, l1_w1, l1_w2, l1_a0, l1_a1, l1_a2, l1_g1, l1_g2, l1_k_k, l1_k_a, l1_r_k, l1_ln_w, l1_ln_b, l1_w_o, l1_ff_w1, l1_ff_w2):
    bp, lp, d = x_prompt.shape
    bs, ls, _ = x_sample.shape
    c_all = _pad_rows(jnp.concatenate([c_prompt, c_sample], axis=0), 2 * SUBLANES)
    m0 = _ada(c_all, l0_ada_w, l0_ada_b)
    m1 = _ada(c_all, l1_ada_w, l1_ada_b)
    mods0_p, mods0_s = _mods(m0[:bp]), _mods(m0[bp:bp + bs])
    mods1_p, mods1_s = _mods(m1[:bp]), _mods(m1[bp:bp + bs])

    w0 = _layer0_weights(l0_w_in, l0_conv_w, l0_a_log, l0_dt_bias, l0_fox_fbias, l0_w_out, l0_ff_w1, l0_ff_w2)
    w1 = _layer1_weights(l1_mu, l1_w_r, l1_w_k, l1_w_v, l1_w0, l1_w1, l1_w2, l1_a0, l1_a1, l1_a2, l1_g1, l1_g2,
                         l1_k_k, l1_k_a, l1_r_k, l1_ln_w, l1_ln_b, l1_w_o, l1_ff_w1, l1_ff_w2)
    norms0 = (l0_norm_mix, l0_norm_ff, l0_gdn_onorm, l0_fox_qnorm, l0_fox_knorm)

    tm_p = min(512, lp)
    y_p, conv_p, delta_p, fox_k_p, fox_v_p, fox_logf_p = _hybrid_layer(
        x_prompt, mods0_p, jnp.zeros((bp, GDN_CONV - 1, GDN_QKV), F32),
        jnp.zeros((bp, GDN_HEADS, GDN_DK, GDN_DV), F32), None, w0, *norms0,
        tm=tm_p, chunk=min(GDN_CHUNK, lp), tb=min(256, lp), blk=min(512, lp))
    y_s, conv_s, delta_s, fox_k_s, fox_v_s, fox_logf_s = _hybrid_layer(
        x_sample, mods0_s, cache_l0_conv, state_l0_delta,
        (cache_l0_fox_k, cache_l0_fox_v, cache_l0_fox_logf), w0, *norms0,
        tm=ls, chunk=min(GDN_CHUNK, ls), tb=ls, blk=None)

    y_p, shift_p, wkv_p = _rwkv_layer(
        y_p, mods1_p, jnp.zeros((bp, d), F32), jnp.zeros((bp, RWKV_HEADS, RWKV_HEAD, RWKV_HEAD), F32),
        w1, l1_norm_mix, l1_norm_ff, tm=tm_p, chunk=min(RWKV_CHUNK, lp), tb=min(512, lp))
    y_s, shift_s, wkv_s = _rwkv_layer(
        y_s, mods1_s, state_l1_shift, state_l1_wkv, w1, l1_norm_mix, l1_norm_ff,
        tm=ls, chunk=min(RWKV_CHUNK, ls), tb=ls)

    return (y_p, y_s, conv_p, conv_s, delta_p, delta_s, fox_k_p, fox_k_s,
            fox_v_p, fox_v_s, fox_logf_p, fox_logf_s, shift_p, shift_s, wkv_p, wkv_s)
```

```python
import functools
import math

import jax
import jax.numpy as jnp
from jax import lax
from jax.experimental import pallas as pl
from jax.experimental.pallas import tpu as pltpu

F32 = jnp.float32
BF16 = jnp.bfloat16

D_MODEL = 1024
D_FF = 4 * D_MODEL
NORM_EPS = 1e-6
LOG2E = 1.4426950408889634

GDN_HEADS = 4
GDN_DK = 128
GDN_DV = 128
GDN_CONV = 4
GDN_QKV = GDN_HEADS * (2 * GDN_DK + GDN_DV)
GDN_CHUNK = 64
SOLVE_HI_LEVELS = 3

FOX_HEADS = 4
FOX_DH = 128
FOX_W = FOX_HEADS * FOX_DH
FOX_SPLIT = 2

RWKV_HEAD = 64
RWKV_HEADS = D_MODEL // RWKV_HEAD
RWKV_GN_EPS = 1e-5 * RWKV_HEAD
RWKV_CHUNK = 64
WKV_PAIRS = 2
WKV_SUB = 256

LANES = 128
SUBLANES = 8
VMEM_LIMIT = 56 * 1024 * 1024

NN = (((1,), (0,)), ((), ()))
NT = (((1,), (1,)), ((), ()))
TN = (((0,), (0,)), ((), ()))


def _dot(a, b, dims=NN):
    return lax.dot_general(a, b, dims, preferred_element_type=F32)


def _mm(a, b, dims=NN):
    return _dot(a.astype(BF16), b.astype(BF16), dims)


def _split2(x):
    hi = x.astype(BF16)
    lo = (x - hi.astype(F32)).astype(BF16)
    return hi, lo


def _split3(x):
    hi = x.astype(BF16)
    r = x - hi.astype(F32)
    mid = r.astype(BF16)
    lo = (r - mid.astype(F32)).astype(BF16)
    return hi, mid, lo


def _mm3(a, b, dims=NN):
    ah, al = _split2(a)
    bh, bl = _split2(b)
    return _dot(ah, bh, dims) + (_dot(ah, bl, dims) + _dot(al, bh, dims))


def _mask_mm(mask01, x):
    x1, x2, x3 = _split3(x)
    return _dot(mask01, x1) + (_dot(mask01, x2) + _dot(mask01, x3))


def _mask_mm2(mask01, x):
    x1, x2 = _split2(x)
    return _dot(mask01, x1) + _dot(mask01, x2)


def _sigmoid(x):
    return 1.0 / (1.0 + jnp.exp(-x))


def _softplus(x):
    return jnp.maximum(x, 0.0) + jnp.log(1.0 + jnp.exp(-jnp.abs(x)))


def _silu(x):
    return x * _sigmoid(x)


def _rms_mod(x, g, sc, sh):
    ms = jnp.mean(x * x, axis=-1, keepdims=True)
    return (x * lax.rsqrt(ms + NORM_EPS) * g) * (1.0 + sc) + sh


def _tri_masks(c):
    row = lax.broadcasted_iota(jnp.int32, (c, c), 0)
    col = lax.broadcasted_iota(jnp.int32, (c, c), 1)
    return row >= col, row > col, row == col


def _cparams(sem):
    return pltpu.CompilerParams(dimension_semantics=sem, vmem_limit_bytes=VMEM_LIMIT)


def _const_spec(shape):
    nd = len(shape)
    return pl.BlockSpec(shape, lambda *_: (0,) * nd, pipeline_mode=pl.Buffered(1))


def _ada_body(c_ref, w_ref, b_ref, o_ref):
    o_ref[...] = _mm3(_silu(c_ref[...]), w_ref[...]) + b_ref[...]


def _ada(c_all, w, b):
    rows, d = c_all.shape
    n = w.shape[1]
    tn = 512
    return pl.pallas_call(
        _ada_body,
        grid=(n // tn,),
        in_specs=[pl.BlockSpec((rows, d), lambda j: (0, 0)),
                  pl.BlockSpec((d, tn), lambda j: (0, j)),
                  pl.BlockSpec((1, tn), lambda j: (0, j))],
        out_specs=pl.BlockSpec((rows, tn), lambda j: (0, j)),
        out_shape=jax.ShapeDtypeStruct((rows, n), F32),
        compiler_params=_cparams(("arbitrary",)),
        name="ada_ln",
    )(c_all, w, b.reshape(1, n))


_IN0_QKV = (0, GDN_QKV)
_IN0_Z = (GDN_QKV, GDN_QKV + 512)
_IN0_Q = (2048, 2560)
_IN0_K = (2560, 3072)
_IN0_V = (3072, 3584)
_IN0_S = (3584, 3712)
_IN0_COLS = 3712


def _inproj0_body(x_ref, sh_ref, sc_ref, g_ref, w_ref, qn_ref, kn_ref, aux_ref,
                  qkv_ref, z_ref, small_ref, qb_ref, kf_ref, vf_ref, kb_ref, vb_ref, ka_ref, cref_ref, carry):
    h = _rms_mod(x_ref[...], g_ref[...], sc_ref[...], sh_ref[...]).astype(BF16)
    proj = lambda c0, c1: _dot(h, w_ref[:, c0:c1])
    third = GDN_QKV // 3
    raw = proj(*_IN0_S)
    q = proj(*_IN0_Q)
    k = proj(*_IN0_K)
    qkv_ref[:, 0:third] = proj(_IN0_QKV[0], _IN0_QKV[0] + third)

    aux = aux_ref[...]
    a_log, dt_bias, f_bias = aux[0:1], aux[1:2], aux[2:3]
    lane = lax.broadcasted_iota(jnp.int32, raw.shape, 1)
    beta = _sigmoid(raw)
    gdec = -jnp.exp(a_log) * _softplus(raw + dt_bias)
    logf = -_softplus(-(raw + f_bias))
    small_ref[...] = jnp.where(lane < 4, beta, jnp.where(lane < 8, gdec, jnp.where(lane < 12, logf, 0.0)))

    @pl.when(pl.program_id(1) == 0)
    def _():
        carry[...] = jnp.zeros_like(carry)

    tm = raw.shape[0]
    tri, _, _ = _tri_masks(tm)
    within = _mask_mm(tri.astype(BF16), logf * LOG2E)
    cref_ref[...] = carry[...] + within[0:1, :]
    carry[...] = carry[...] + within[tm - 1:tm, :]
    bias = within[0:1, :] - within
    lane_h = lax.broadcasted_iota(jnp.int32, (tm, FOX_DH), 1)
    for hd in range(FOX_HEADS):
        col = bias[:, 8 + hd:9 + hd]
        b1 = col.astype(BF16).astype(F32)
        b2 = (col - b1).astype(BF16).astype(F32)
        b3 = col - b1 - b2
        cols = jnp.where(lane_h == 0, b1, jnp.where(lane_h == 1, b2, jnp.where(lane_h == 2, b3, 0.0)))
        ka_ref[:, hd * FOX_DH:(hd + 1) * FOX_DH] = cols.astype(BF16)

    qkv_ref[:, third:2 * third] = proj(_IN0_QKV[0] + third, _IN0_QKV[0] + 2 * third)
    qn = qn_ref[...]
    for hd in range(FOX_HEADS):
        sl = slice(hd * FOX_DH, (hd + 1) * FOX_DH)
        qh = q[:, sl]
        qh = qh * lax.rsqrt(jnp.mean(qh * qh, axis=-1, keepdims=True) + NORM_EPS) * qn * (FOX_DH ** -0.5 * LOG2E)
        qb_ref[:, sl] = qh.astype(BF16)

    qkv_ref[:, 2 * third:] = proj(_IN0_QKV[0] + 2 * third, _IN0_QKV[1])
    kn = kn_ref[...]
    for hd in range(FOX_HEADS):
        sl = slice(hd * FOX_DH, (hd + 1) * FOX_DH)
        kh = k[:, sl]
        kh = kh * lax.rsqrt(jnp.mean(kh * kh, axis=-1, keepdims=True) + NORM_EPS) * kn
        kf_ref[:, sl] = kh
        kb_ref[:, sl] = kh.astype(BF16)

    z_ref[...] = proj(*_IN0_Z)
    v = proj(*_IN0_V)
    vf_ref[...] = v
    vb_ref[...] = v.astype(BF16)


def _inproj0(x, sh, sc, g, w_all, qn, kn, aux, tm):
    b, l, d = x.shape
    if b > 1 and tm == l:
        (xf,), (shf, scf) = _flatten_streams([x], [sh, sc])
        outs = _inproj0(xf, shf, scf, g, w_all, qn, kn, aux, b * l)
        return [o.reshape(b, l, o.shape[-1]) for o in outs[:-1]] + [outs[-1]]
    row = lambda width: pl.BlockSpec((None, tm, width), lambda bi, i: (bi, i, 0))
    mod = _mod_spec(sh, tm)
    out = lambda width, dt: jax.ShapeDtypeStruct((b, l, width), dt)
    return pl.pallas_call(
        _inproj0_body,
        grid=(b, l // tm),
        in_specs=[row(d), mod, mod, _const_spec((1, d)), _const_spec((d, _IN0_COLS)),
                  _const_spec((1, FOX_DH)), _const_spec((1, FOX_DH)), _const_spec((SUBLANES, LANES))],
        out_specs=[row(GDN_QKV), row(512), row(LANES), row(FOX_W), row(FOX_W), row(FOX_W), row(FOX_W), row(FOX_W),
                   row(FOX_W), pl.BlockSpec((None, None, 1, LANES), lambda bi, i: (bi, i, 0, 0))],
        out_shape=[out(GDN_QKV, F32), out(512, F32), out(LANES, F32), out(FOX_W, BF16),
                   out(FOX_W, F32), out(FOX_W, F32), out(FOX_W, BF16), out(FOX_W, BF16), out(FOX_W, BF16),
                   jax.ShapeDtypeStruct((b, l // tm, 1, LANES), F32)],
        scratch_shapes=[pltpu.VMEM((1, LANES), F32)],
        compiler_params=_cparams(("arbitrary", "arbitrary")),
        name="l0_in_proj",
    )(x, sh, sc, g, w_all, qn, kn, aux)


def _chunk_masks(tb, chunk):
    shift = int(math.log2(chunk))
    row = lax.broadcasted_iota(jnp.int32, (tb, tb), 0)
    col = lax.broadcasted_iota(jnp.int32, (tb, tb), 1)
    same = (row >> shift) == (col >> shift)
    return same & (row >= col), same & (row > col), row == col


def _gdn_body(qkv_ref, z_ref, small_ref, cinit_ref, s0_ref, cw_ref, on_ref,
              o_ref, sout_ref, ext, state, *, chunk):
    c = chunk
    tb = qkv_ref.shape[0]
    ci = pl.program_id(1)

    @pl.when(ci == 0)
    def _():
        ext[0:SUBLANES, :] = cinit_ref[...]
        state[...] = s0_ref[...]

    ext[SUBLANES:SUBLANES + tb, :] = qkv_ref[...]
    cw = cw_ref[...]
    base = SUBLANES - (GDN_CONV - 1)
    y = ext[base:base + tb, :] * cw[0:1]
    for i in range(1, GDN_CONV):
        y = y + ext[base + i:base + i + tb, :] * cw[i:i + 1]
    qkv = _silu(y)
    ext[0:SUBLANES, :] = ext[tb:tb + SUBLANES, :]

    tri, strict, eye = _chunk_masks(tb, c)
    sm = small_ref[...]
    gc_all = _mask_mm(tri.astype(BF16), sm)
    onorm = on_ref[...]
    eye_d = _tri_masks(GDN_DK)[2]
    heads = []
    for hd in range(GDN_HEADS):
        q = qkv[:, hd * GDN_DK:(hd + 1) * GDN_DK]
        k = qkv[:, 512 + hd * GDN_DK:512 + (hd + 1) * GDN_DK]
        v = qkv[:, 1024 + hd * GDN_DV:1024 + (hd + 1) * GDN_DV]
        q = q * lax.rsqrt(jnp.sum(q * q, axis=-1, keepdims=True) + NORM_EPS) * GDN_DK ** -0.5
        k = k * lax.rsqrt(jnp.sum(k * k, axis=-1, keepdims=True) + NORM_EPS)
        beta = sm[:, hd:hd + 1]
        gc = gc_all[:, 4 + hd:5 + hd]
        gc_row = jnp.sum(jnp.where(eye, gc, 0.0), axis=0, keepdims=True)
        dec = jnp.where(tri, jnp.exp(jnp.where(tri, gc - gc_row, 0.0)), 0.0)
        kb = k.astype(BF16)
        egc = jnp.exp(gc)
        heads.append(dict(k=k, qe=q * egc,
                          n=-jnp.where(strict, beta * _dot(kb, kb, NT) * dec, 0.0),
                          sol=jnp.concatenate([k * (beta * egc), v * beta], axis=1),
                          qk=_dot(q.astype(BF16), kb, NT) * dec))
    levels = max(1, int(math.ceil(math.log2(c))))
    for lvl in range(levels):
        mm = _mm3 if lvl < SOLVE_HI_LEVELS else _mm
        for x in heads:
            x["sol"] = x["sol"] + mm(x["n"], x["sol"])
        if lvl + 1 < levels:
            for x in heads:
                x["n"] = mm(x["n"], x["n"])
    for x in heads:
        qs = _mm(x["qk"], x["sol"])
        x["oq"] = x["qe"] - qs[:, :GDN_DK]
        x["o0"] = qs[:, GDN_DK:]
    s_hs = [state[hd] for hd in range(GDN_HEADS)]
    for c0 in range(0, tb, c):
        rows = slice(c0, c0 + c)
        for hd, x in enumerate(heads):
            g_last = gc_all[c0 + c - 1:c0 + c, 4 + hd:5 + hd]
            ks = _mm3(x["k"][rows] * jnp.exp(g_last - gc_all[rows, 4 + hd:5 + hd]), x["sol"][rows], TN)
            o = _mm3(x["oq"][rows], s_hs[hd]) + x["o0"][rows]
            s_hs[hd] = _mm3(jnp.where(eye_d, jnp.exp(g_last), 0.0) - ks[:, :GDN_DK], s_hs[hd]) + ks[:, GDN_DK:]
            zz = z_ref[rows, hd * GDN_DV:(hd + 1) * GDN_DV]
            o = o * lax.rsqrt(jnp.mean(o * o, axis=-1, keepdims=True) + NORM_EPS) * onorm
            o_ref[rows, hd * GDN_DV:(hd + 1) * GDN_DV] = (o * _silu(zz)).astype(BF16)
    for hd in range(GDN_HEADS):
        state[hd] = s_hs[hd]

    @pl.when(ci == pl.num_programs(1) - 1)
    def _():
        sout_ref[...] = state[...]


def _gdn(qkv_raw, z, small, conv_init8, s0, conv_w8, onorm, chunk, tb):
    b, l, _ = qkv_raw.shape
    row = lambda width: pl.BlockSpec((None, tb, width), lambda bi, i: (bi, i, 0))
    st = pl.BlockSpec((None, GDN_HEADS, GDN_DK, GDN_DV), lambda bi, i: (bi, 0, 0, 0))
    return pl.pallas_call(
        functools.partial(_gdn_body, chunk=chunk),
        grid=(b, l // tb),
        in_specs=[row(GDN_QKV), row(512), row(LANES),
                  pl.BlockSpec((None, SUBLANES, GDN_QKV), lambda bi, i: (bi, 0, 0)), st,
                  _const_spec((SUBLANES, GDN_QKV)), _const_spec((1, GDN_DV))],
        out_specs=[row(512), st],
        out_shape=[jax.ShapeDtypeStruct((b, l, 512), BF16),
                   jax.ShapeDtypeStruct((b, GDN_HEADS, GDN_DK, GDN_DV), F32)],
        scratch_shapes=[pltpu.VMEM((tb + SUBLANES, GDN_QKV), F32),
                        pltpu.VMEM((GDN_HEADS, GDN_DK, GDN_DV), F32)],
        compiler_params=_cparams(("arbitrary", "arbitrary")),
        name="gdn_mixer",
    )(qkv_raw, z, small, conv_init8, s0, conv_w8, onorm)


def _cumsum_body(x_ref, o_ref):
    rows = x_ref.shape[1]
    li = lax.broadcasted_iota(jnp.int32, (LANES, LANES), 0)
    lj = lax.broadcasted_iota(jnp.int32, (LANES, LANES), 1)
    upper = (li <= lj).astype(BF16)
    ones = jnp.ones((LANES, LANES), BF16)
    ri = lax.broadcasted_iota(jnp.int32, (rows, rows), 0)
    rj = lax.broadcasted_iota(jnp.int32, (rows, rows), 1)
    before = (rj < ri).astype(BF16)
    for g in range(x_ref.shape[0]):
        x1, x2, x3 = _split3(x_ref[g])
        within = _dot(x1, upper) + (_dot(x2, upper) + _dot(x3, upper))
        total = _dot(x1, ones) + (_dot(x2, ones) + _dot(x3, ones))
        o_ref[g] = within + _mask_mm(before, total)


def _cumsum_rows(x):
    g, rows, _ = x.shape
    per_step = math.gcd(g, SUBLANES)
    spec = pl.BlockSpec((per_step, rows, LANES), lambda i: (i, 0, 0))
    return pl.pallas_call(
        _cumsum_body, grid=(g // per_step,), in_specs=[spec], out_specs=spec,
        out_shape=jax.ShapeDtypeStruct(x.shape, F32),
        compiler_params=_cparams(("arbitrary",)),
        name="logf_cumsum",
    )(x)


def _fox_prompt_body(cref_ref, q_ref, k_ref, ka_ref, v_ref, o_ref, s_a, s_b, m_scr, acc_scr, *, blk):
    hd = pl.program_id(0)
    qi = pl.program_id(1)
    sub = blk // FOX_SPLIT
    half = blk // 2
    lane = lax.broadcasted_iota(jnp.int32, (blk, FOX_DH), 1)
    ones = jnp.ones((half, FOX_DH), BF16)
    q = jnp.concatenate([q_ref[...], jnp.where(lane < 3, 1.0, 0.0).astype(BF16)], axis=1)
    c0 = cref_ref[hd, qi]
    row = lax.broadcasted_iota(jnp.int32, (sub, half), 0)
    col = lax.broadcasted_iota(jnp.int32, (sub, half), 1)
    m_scr[...] = jnp.full_like(m_scr, -jnp.inf)
    acc_scr[...] = jnp.zeros_like(acc_scr)

    def visibility(i, first_col):
        if first_col is None or first_col + half - 1 <= i * sub:
            return "all"
        return "none" if first_col > (i + 1) * sub - 1 else "causal"

    def scores(h, s_ref, first_col=None):
        rows = pl.ds(pl.multiple_of(h * half, half), half)
        kh = jnp.concatenate([k_ref[rows, :], ka_ref[rows, :]], axis=1)
        for i in range(FOX_SPLIT):
            if visibility(i, first_col) != "none":
                s_ref[i * sub:(i + 1) * sub, :] = _dot(q[i * sub:(i + 1) * sub], kh, NT)

    def consume(h, s_ref, delta, first_col):
        rows = pl.ds(pl.multiple_of(h * half, half), half)
        vh = jnp.concatenate([v_ref[rows, :], ones], axis=1)
        for i in range(FOX_SPLIT):
            if visibility(i, first_col) == "none":
                continue
            rs = slice(i * sub, (i + 1) * sub)
            s = s_ref[rs, :]
            if visibility(i, first_col) == "causal":
                s = jnp.where(col + first_col <= row + i * sub, s, -jnp.inf)
            m = m_scr[rs, :]
            m_new = jnp.maximum(m, jnp.max(s, axis=-1, keepdims=True) + delta)
            shift = m_new - delta
            p = jnp.concatenate([jnp.exp2(s[:, g * LANES:(g + 1) * LANES] - shift) for g in range(half // LANES)],
                                axis=1)
            pv = _dot(p.astype(BF16), vh)
            alpha = jnp.exp2(m - m_new)
            acc = acc_scr[rs, :]
            acc_scr[rs, :] = jnp.concatenate(
                [alpha * acc[:, g * LANES:(g + 1) * LANES] + pv[:, g * LANES:(g + 1) * LANES] for g in range(2)],
                axis=1)
            m_scr[rs, :] = m_new

    def past_block(j):
        delta = c0 - cref_ref[hd, j]
        scores(2 * j + 1, s_b)
        consume(2 * j, s_a, delta, None)
        scores(2 * j + 2, s_a)
        consume(2 * j + 1, s_b, delta, None)

    def two_past_blocks(jj, carry):
        past_block(2 * jj)
        past_block(2 * jj + 1)
        return carry

    scores(0, s_a)
    lax.fori_loop(0, qi // 2, two_past_blocks, 0)

    @pl.when(qi % 2 == 1)
    def _():
        past_block(qi - 1)

    scores(2 * qi + 1, s_b, half)
    consume(2 * qi, s_a, 0.0, 0)
    consume(2 * qi + 1, s_b, 0.0, half)
    acc = acc_scr[...]
    o_ref[...] = (acc[:, :FOX_DH] / acc[:, FOX_DH:FOX_DH + 1]).astype(BF16)


def _fox_prompt(q, k, ka, v, cref, blk):
    l = q.shape[0]
    nb = l // blk
    res = pl.BlockSpec((l, FOX_DH), lambda h, i: (0, h))
    return pl.pallas_call(
        functools.partial(_fox_prompt_body, blk=blk),
        grid=(FOX_HEADS, nb),
        in_specs=[pl.BlockSpec(memory_space=pltpu.SMEM),
                  pl.BlockSpec((blk, FOX_DH), lambda h, i: (i, h)), res, res, res],
        out_specs=pl.BlockSpec((blk, FOX_DH), lambda h, i: (i, h)),
        out_shape=jax.ShapeDtypeStruct((l, FOX_W), BF16),
        scratch_shapes=[pltpu.VMEM((blk, blk // 2), F32), pltpu.VMEM((blk, blk // 2), F32),
                        pltpu.VMEM((blk, LANES), F32), pltpu.VMEM((blk, 2 * FOX_DH), F32)],
        compiler_params=_cparams(("arbitrary", "arbitrary")),
        name="fox_prompt_attn",
    )(cref, q, k, ka, v)


def _fox_step_body(q_ref, kn_ref, vn_ref, kc_ref, vc_ref, cc_ref, cnr_ref, cnc_ref, o_ref):
    q = q_ref[...]
    n = q.shape[0]
    c_i = cnc_ref[...]
    s_c = _mm(q, kc_ref[...], NT) + (c_i - cc_ref[...]) * LOG2E
    s_n = _mm(q, kn_ref[...], NT) + (c_i - cnr_ref[...]) * LOG2E
    tri, _, _ = _tri_masks(n)
    s_n = jnp.where(tri, s_n, -jnp.inf)
    m = jnp.maximum(jnp.max(s_c, axis=-1, keepdims=True), jnp.max(s_n, axis=-1, keepdims=True))
    p_c = jnp.exp2(s_c - m)
    p_n = jnp.exp2(s_n - m)
    l = jnp.sum(p_c, axis=-1, keepdims=True) + jnp.sum(p_n, axis=-1, keepdims=True)
    o = _mm(p_c, vc_ref[...]) + _mm(p_n, vn_ref[...])
    o_ref[...] = (o / l).astype(BF16)


def _fox_step(q, k_new, v_new, k_cache, v_cache, c_cache, c_new_row, c_new_col):
    b, n, _ = q.shape
    p = k_cache.shape[1]
    new = pl.BlockSpec((None, n, FOX_DH), lambda bi, h: (bi, 0, h))
    cache = pl.BlockSpec((None, p, FOX_DH), lambda bi, h: (bi, 0, h))
    return pl.pallas_call(
        _fox_step_body,
        grid=(b, FOX_HEADS),
        in_specs=[new, new, new, cache, cache,
                  pl.BlockSpec((None, None, 1, p), lambda bi, h: (bi, h, 0, 0)),
                  pl.BlockSpec((None, None, 1, n), lambda bi, h: (bi, h, 0, 0)),
                  pl.BlockSpec((None, None, n, 1), lambda bi, h: (bi, h, 0, 0))],
        out_specs=new,
        out_shape=jax.ShapeDtypeStruct((b, n, FOX_W), BF16),
        compiler_params=_cparams(("arbitrary", "arbitrary")),
        name="fox_step_attn",
    )(q, k_new, v_new, k_cache, v_cache, c_cache, c_new_row, c_new_col)


def _resid_mlp(x, mix, gt1, sh2, sc2, gt2, g2, w1_ref, w2_ref):
    x1 = x + gt1 * mix
    h2 = _rms_mod(x1, g2, sc2, sh2).astype(BF16)
    acc = jnp.zeros_like(x1)
    step = D_MODEL
    for c0 in range(0, D_FF, step):
        hid = jnp.maximum(_dot(h2, w1_ref[:, c0:c0 + step]), 0.0)
        acc = acc + _dot((hid * hid).astype(BF16), w2_ref[c0:c0 + step, :])
    return x1 + gt2 * acc


def _mix0_mlp_body(x_ref, oa_ref, ob_ref, gt1, sh2, sc2, gt2, g2, wo_ref, w1_ref, w2_ref, out_ref):
    half = GDN_HEADS * GDN_DV
    mix = _dot(oa_ref[...], wo_ref[0:half, :]) + _dot(ob_ref[...], wo_ref[half:, :])
    out_ref[...] = _resid_mlp(x_ref[...], mix, gt1[...], sh2[...], sc2[...], gt2[...], g2[...], w1_ref, w2_ref)


def _mix1_mlp_body(x_ref, y_ref, gate_ref, gt1, sh2, sc2, gt2, g2, wo_ref, w1_ref, w2_ref, out_ref):
    mix = _mm(y_ref[...] * gate_ref[...], wo_ref[...])
    out_ref[...] = _resid_mlp(x_ref[...], mix, gt1[...], sh2[...], sc2[...], gt2[...], g2[...], w1_ref, w2_ref)


def _flatten_streams(arrays, mods):
    b, l, d = arrays[0].shape
    flat = [a.reshape(1, b * l, a.shape[-1]) for a in arrays]
    per_row = [jnp.broadcast_to(m, (b, l, d)).reshape(1, b * l, d) for m in mods]
    return flat, per_row


def _mod_spec(m, tm):
    d = m.shape[-1]
    if m.shape[1] == 1:
        return pl.BlockSpec((None, 1, d), lambda bi, i: (bi, 0, 0))
    return pl.BlockSpec((None, tm, d), lambda bi, i: (bi, i, 0))


def _mix_mlp(body, name, x, mix_in, mods, g2, wo, w1, w2, tm):
    b, l, d = x.shape
    if b > 1 and tm == l:
        flat, per_row = _flatten_streams([x] + mix_in, mods)
        return _mix_mlp(body, name, flat[0], flat[1:], per_row, g2, wo, w1, w2, b * l).reshape(b, l, d)
    row = lambda width: pl.BlockSpec((None, tm, width), lambda bi, i: (bi, i, 0))
    mod = _mod_spec(mods[0], tm)
    return pl.pallas_call(
        body,
        grid=(b, l // tm),
        in_specs=[row(d)] + [row(a.shape[-1]) for a in mix_in] + [mod] * 4
                 + [_const_spec((1, d)), _const_spec(wo.shape), _const_spec(w1.shape), _const_spec(w2.shape)],
        out_specs=row(d),
        out_shape=jax.ShapeDtypeStruct((b, l, d), F32),
        compiler_params=_cparams(("arbitrary", "arbitrary")),
        name=name,
    )(x, *mix_in, *mods, g2, wo, w1, w2)


def _rwkv_proj_body(x_ref, shp_ref, sh_ref, sc_ref, g_ref, mu_ref, vec_ref,
                    wr_ref, wk_ref, wv_ref, w1_ref, w2_ref, a1_ref, a2_ref, g1_ref, g2_ref,
                    r_ref, lw_ref, k_ref, v_ref, a_ref, gate_ref, shift_ref, hext):
    tm = x_ref.shape[0]
    i = pl.program_id(1)
    h = _rms_mod(x_ref[...], g_ref[...], sc_ref[...], sh_ref[...])

    @pl.when(i == 0)
    def _():
        hext[SUBLANES - 1:SUBLANES, :] = shp_ref[...]

    hext[SUBLANES:SUBLANES + tm, :] = h
    h_prev = hext[SUBLANES - 1:SUBLANES - 1 + tm, :]
    hext[SUBLANES - 1:SUBLANES, :] = h[tm - 1:tm, :]
    shift_ref[...] = h[tm - 1:tm, :]

    xx = h_prev - h
    mu = mu_ref[...]
    vec = vec_ref[...]
    w0, a0 = vec[0:1], vec[1:2]
    mixed = lambda j: (h + xx * mu[j:j + 1]).astype(BF16)
    x_r = mixed(0)
    x_k = mixed(2)
    r_ref[...] = _dot(x_r, wr_ref[...])
    x_v = mixed(3)
    k_ref[...] = _dot(x_k, wk_ref[...])
    x_w = mixed(1)
    v_ref[...] = _dot(x_v, wv_ref[...])
    x_a = mixed(4)
    t_w = _dot(x_w, w1_ref[...])
    x_g = mixed(5)
    t_a = _dot(x_a, a1_ref[...])
    t_g = _dot(x_g, g1_ref[...])
    wl = w0 + _mm(jnp.tanh(t_w), w2_ref[...])
    a_ref[...] = _sigmoid(a0 + _mm(t_a, a2_ref[...]))
    lw_ref[...] = -jnp.exp(-_softplus(-wl) - 0.5)
    gate_ref[...] = _mm(_sigmoid(t_g), g2_ref[...])


def _rwkv_proj(x, shift_prev, sh, sc, g, mu8, vec8, weights, tm):
    b, l, d = x.shape
    row = pl.BlockSpec((None, tm, d), lambda bi, i: (bi, i, 0))
    mod = pl.BlockSpec((None, 1, d), lambda bi, i: (bi, 0, 0))
    big = jax.ShapeDtypeStruct((b, l, d), F32)
    return pl.pallas_call(
        _rwkv_proj_body,
        grid=(b, l // tm),
        in_specs=[row, mod, mod, mod, _const_spec((1, d)), _const_spec((SUBLANES, d)), _const_spec((SUBLANES, d))]
                 + [_const_spec(w.shape) for w in weights],
        out_specs=[row] * 6 + [mod],
        out_shape=[big] * 6 + [jax.ShapeDtypeStruct((b, 1, d), F32)],
        scratch_shapes=[pltpu.VMEM((tm + SUBLANES, d), F32)],
        compiler_params=_cparams(("arbitrary", "arbitrary")),
        name="rwkv_proj",
    )(x, shift_prev, sh, sc, g, mu8, vec8, *weights)


def _pair_sum(x, first):
    lo = jnp.sum(jnp.where(first, x, 0.0), axis=-1, keepdims=True)
    hi = jnp.sum(jnp.where(first, 0.0, x), axis=-1, keepdims=True)
    return jnp.where(first, lo, hi)


def _wkv_body(r_ref, lw_ref, k_ref, v_ref, a_ref, vec_ref, s0_ref, y_ref, sout_ref, state, *, chunk):
    c = chunk
    tb = r_ref.shape[0]
    n = RWKV_HEAD
    t = pl.program_id(2)
    pairs = r_ref.shape[1] // LANES

    @pl.when(t == 0)
    def _():
        state[...] = jnp.zeros_like(state)
        for p in range(pairs):
            state[p, 0:n, 0:n] = s0_ref[2 * p]
            state[p, n:, n:] = s0_ref[2 * p + 1]

    ts = min(tb, WKV_SUB)
    tri, strict, eye = _chunk_masks(ts, c)
    first = lax.broadcasted_iota(jnp.int32, (ts, LANES), 1) < n
    hr = lax.broadcasted_iota(jnp.int32, (LANES, LANES), 0)
    hc = lax.broadcasted_iota(jnp.int32, (LANES, LANES), 1)
    same_head = (hr < n) == (hc < n)
    eye_l = jnp.where(hr == hc, 1.0, 0.0)

    fc = lax.broadcasted_iota(jnp.int32, (c, LANES), 1) < n
    levels = max(1, int(math.ceil(math.log2(c))))
    lanes = [slice(p * LANES, (p + 1) * LANES) for p in range(pairs)]
    subs = [slice(t0, t0 + ts) for t0 in range(0, tb, ts)]

    ctx = [_wkv_scores(r_ref[rs, ls], lw_ref[rs, ls], k_ref[rs, ls], v_ref[rs, ls], a_ref[rs, ls], vec_ref[:, ls],
                       c, tri, strict, eye, first) for rs in subs for ls in lanes]
    for _ in range(1, levels):
        for x in ctx:
            x["nks"] = [_mm(nk, nk) for nk in x["nks"]]
        for x in ctx:
            x["t_invs"] = [t_inv + _mm(nk, t_inv) for nk, t_inv in zip(x["nks"], x["t_invs"])]
    for x in ctx:
        x["ws"] = [_mm(t_inv, rhs) for t_inv, rhs in zip(x["t_invs"], x["rhs"])]
    for x in ctx:
        _wkv_state_free_terms(x, first)

    sps = [state[p] for p in range(pairs)]
    for c0 in range(0, tb, c):
        sub, off = divmod(c0, ts)
        rows = slice(off, off + c)
        for p in range(pairs):
            x = ctx[sub * pairs + p]
            gam = jnp.exp(x["cl"][off + c - 1:off + c, :])
            gp = jnp.where(same_head, eye_l + _mm(x["wa"][rows], x["bt"][rows], TN), 0.0) * gam
            hp = jnp.where(same_head,
                           _dot(jnp.concatenate([x["u0"][rows], x["v"][rows]], axis=0).astype(BF16),
                                jnp.concatenate([x["bt"][rows], x["kt"][rows]], axis=0).astype(BF16), TN), 0.0) * gam
            y = _mm(x["rq"][rows], sps[p], NT) + x["y0"][rows]
            sps[p] = _mm(sps[p], gp) + hp
            yc = y - _pair_sum(y, fc) * (1.0 / n)
            var = _pair_sum(yc * yc, fc) * (1.0 / n)
            y_ref[c0:c0 + c, lanes[p]] = (yc * lax.rsqrt(var + RWKV_GN_EPS) * x["ln_w"] + x["ln_b"]
                                          + x["bonus"][rows])
    for p in range(pairs):
        state[p] = sps[p]

    @pl.when(t == pl.num_programs(2) - 1)
    def _():
        for p in range(pairs):
            sout_ref[2 * p] = state[p, 0:n, 0:n]
            sout_ref[2 * p + 1] = state[p, n:, n:]


def _wkv_scores(r, lw, k, v, a, vec, c, tri, strict, eye, first):
    tb = r.shape[0]
    k_k, k_a, r_k, ln_w, ln_b = vec[0:1], vec[1:2], vec[2:3], vec[3:4], vec[4:5]
    kq = k * k_k
    kkn = kq * lax.rsqrt(_pair_sum(kq * kq, first) + NORM_EPS)
    kp = k * (1.0 + (a - 1.0) * k_a)
    cl = _mask_mm2(tri.astype(BF16), lw)
    e_neg = jnp.exp(-cl)
    at = -kkn * jnp.exp(cl - lw)
    rt = r * jnp.exp(cl)
    bt = (kkn * a) * e_neg
    kt = kp * e_neg
    kcat = jnp.concatenate([bt, kt], axis=0).astype(BF16)
    vb = v.astype(BF16)
    x = dict(cl=cl, rt=rt, bt=bt, kt=kt, v=v, ln_w=ln_w, ln_b=ln_b,
             bonus=_pair_sum(r * kp * r_k, first) * v, arbs=[], avrs=[], nks=[], t_invs=[], rhs=[])
    for head_lanes in (first, jnp.logical_not(first)):
        at_e = jnp.where(head_lanes, at, 0.0)
        ma = _dot(at_e.astype(BF16), kcat, NT)
        mr = _dot(jnp.where(head_lanes, rt, 0.0).astype(BF16), kcat, NT)
        nk = jnp.where(strict, ma[:, :tb], 0.0)
        av_a = _dot(jnp.where(strict, ma[:, tb:], 0.0).astype(BF16), vb)
        x["avrs"].append(_dot(jnp.where(tri, mr[:, tb:], 0.0).astype(BF16), vb))
        x["arbs"].append(jnp.where(tri, mr[:, :tb], 0.0).astype(BF16))
        x["nks"].append(nk)
        x["t_invs"].append(jnp.where(eye, 1.0, nk))
        x["rhs"].append(jnp.concatenate([at_e, av_a], axis=1))
    return x


def _wkv_state_free_terms(x, first):
    ws, arbs, avrs = x["ws"], x["arbs"], x["avrs"]
    x["wa"] = ws[0][:, :LANES] + ws[1][:, :LANES]
    x["u0"] = jnp.where(first, ws[0][:, LANES:], ws[1][:, LANES:])
    wcat = jnp.concatenate([x["wa"], x["u0"]], axis=1).astype(BF16)
    aw0 = _dot(arbs[0], wcat)
    aw1 = _dot(arbs[1], wcat)
    x["rq"] = x["rt"] + jnp.where(first, aw0[:, :LANES], aw1[:, :LANES])
    x["y0"] = jnp.where(first, aw0[:, LANES:] + avrs[0], aw1[:, LANES:] + avrs[1])


def _wkv(r, lw, k, v, a, vec8, s0, chunk, tb, pairs):
    b, l, d = r.shape
    width = pairs * LANES
    heads = width // RWKV_HEAD
    row = pl.BlockSpec((None, tb, width), lambda bi, p, t: (bi, t, p))
    st = pl.BlockSpec((None, heads, RWKV_HEAD, RWKV_HEAD), lambda bi, p, t: (bi, p, 0, 0))
    return pl.pallas_call(
        functools.partial(_wkv_body, chunk=chunk),
        grid=(b, d // width, l // tb),
        in_specs=[row] * 5 + [pl.BlockSpec((SUBLANES, width), lambda bi, p, t: (0, p)), st],
        out_specs=[row, st],
        out_shape=[jax.ShapeDtypeStruct((b, l, d), F32),
                   jax.ShapeDtypeStruct((b, RWKV_HEADS, RWKV_HEAD, RWKV_HEAD), F32)],
        scratch_shapes=[pltpu.VMEM((pairs, LANES, LANES), F32)],
        compiler_params=_cparams(("arbitrary", "arbitrary", "arbitrary")),
        name="wkv7_chunked",
    )(r, lw, k, v, a, vec8, s0)


def _pad_rows(a, rows):
    return jnp.pad(a, ((0, rows - a.shape[0]), (0, 0)))


def _mods(m_rows):
    return [t[:, None, :] for t in jnp.split(m_rows, 6, axis=-1)]


def _layer0_weights(w_in, conv_w, a_log, dt_bias, fox_fbias, w_out, ff_w1, ff_w2):
    c = (0, 1536, 2048, 2052, 2056, 2568, 3080, 3592, 3596)
    seg = lambda i: w_in[:, c[i]:c[i + 1]]
    small = jnp.concatenate([seg(2), seg(3), seg(7)], axis=1)
    small = jnp.pad(small, ((0, 0), (0, LANES - small.shape[1])))
    w_all = jnp.concatenate([seg(0), seg(1), seg(4), seg(5), seg(6), small], axis=1).astype(BF16)
    aux = jnp.zeros((SUBLANES, LANES), F32)
    aux = aux.at[0, 4:8].set(a_log).at[1, 4:8].set(dt_bias).at[2, 8:12].set(fox_fbias)
    return dict(w_all=w_all, aux=aux, conv_w8=_pad_rows(conv_w, SUBLANES),
                w_out=w_out.astype(BF16), ff_w1=ff_w1.astype(BF16), ff_w2=ff_w2.astype(BF16))


def _hybrid_layer(x, mods, conv_buf, s0, fox_cache, wts, norm_mix, norm_ff, gdn_onorm, fox_qnorm, fox_knorm,
                  tm, chunk, tb, blk):
    b, l, d = x.shape
    sh1, sc1, gt1, sh2, sc2, gt2 = mods
    row = lambda g: g.reshape(1, -1)
    qkv_raw, z, small, q_b, k_f, v_f, k_b, v_b, k_a, cref = _inproj0(
        x, sh1, sc1, row(norm_mix), wts["w_all"], row(fox_qnorm), row(fox_knorm), wts["aux"], tm)
    conv_new = qkv_raw[:, l - (GDN_CONV - 1):, :]
    logf = small[:, :, 8:12]
    conv_init8 = jnp.pad(conv_buf, ((0, 0), (SUBLANES - (GDN_CONV - 1), 0), (0, 0)))
    o_a, s_new = _gdn(qkv_raw, z, small, conv_init8, s0, wts["conv_w8"], row(gdn_onorm), chunk, tb)

    if fox_cache is None:
        assert blk == tm and b == 1
        o_b = _fox_prompt(q_b[0], k_b[0], k_a[0], v_b[0], jnp.transpose(cref[0, :, 0, 8:8 + FOX_HEADS]), blk)[None]
    else:
        k_cache, v_cache, logf_cache = fox_cache
        p = k_cache.shape[1]
        total = p + l
        padded = -(-total // (SUBLANES * LANES)) * (SUBLANES * LANES)
        lf = jnp.concatenate([jnp.transpose(logf_cache, (0, 2, 1)), jnp.transpose(logf, (0, 2, 1))], axis=2)
        lf = jnp.pad(lf, ((0, 0), (0, 0), (padded - total, 0)))
        cs = _cumsum_rows(lf.reshape(b * FOX_HEADS, padded // LANES, LANES)).reshape(b, FOX_HEADS, padded)
        c_cache = cs[:, :, None, padded - total:padded - l]
        c_new = cs[:, :, padded - l:]
        o_b = _fox_step(q_b, k_f, v_f, k_cache.reshape(b, p, FOX_W), v_cache.reshape(b, p, FOX_W),
                        c_cache, c_new[:, :, None, :], c_new[:, :, :, None])

    x = _mix_mlp(_mix0_mlp_body, "l0_out_mlp", x, [o_a, o_b], [gt1, sh2, sc2, gt2], row(norm_ff),
                 wts["w_out"], wts["ff_w1"], wts["ff_w2"], tm)
    return (x, conv_new, s_new, k_f.reshape(b, l, FOX_HEADS, FOX_DH), v_f.reshape(b, l, FOX_HEADS, FOX_DH), logf)


def _layer1_weights(mu, w_r, w_k, w_v, w0, w1, w2, a0, a1, a2, g1, g2, k_k, k_a, r_k, ln_w, ln_b, w_o, ff_w1, ff_w2):
    padc = lambda w: jnp.pad(w, ((0, 0), (0, -w.shape[1] % LANES))).astype(BF16)
    padr = lambda w: jnp.pad(w, ((0, -w.shape[0] % LANES), (0, 0))).astype(BF16)
    proj = [w_r.astype(BF16), w_k.astype(BF16), w_v.astype(BF16),
            padc(w1), padr(w2), padc(a1), padr(a2), padc(g1), padr(g2)]
    return dict(mu8=_pad_rows(mu, SUBLANES), pvec=_pad_rows(jnp.stack([w0, a0]), SUBLANES),
                svec=_pad_rows(jnp.stack([k_k, k_a, r_k.reshape(-1), ln_w, ln_b]), SUBLANES),
                proj=proj, w_o=w_o.astype(BF16), ff_w1=ff_w1.astype(BF16), ff_w2=ff_w2.astype(BF16))


def _rwkv_layer(x, mods, shift_prev, s0, wts, norm_mix, norm_ff, tm, chunk, tb, pairs):
    sh1, sc1, gt1, sh2, sc2, gt2 = mods
    row = lambda g: g.reshape(1, -1)
    r, lw, k, v, a, gate, shift_new = _rwkv_proj(
        x, shift_prev[:, None, :], sh1, sc1, row(norm_mix), wts["mu8"], wts["pvec"], wts["proj"], tm)
    y, s_new = _wkv(r, lw, k, v, a, wts["svec"], s0, chunk, tb, pairs)
    x = _mix_mlp(_mix1_mlp_body, "l1_out_mlp", x, [y, gate], [gt1, sh2, sc2, gt2], row(norm_ff),
                 wts["w_o"], wts["ff_w1"], wts["ff_w2"], tm)
    return x, shift_new[:, 0, :], s_new


def kernel(x_prompt, x_sample, c_prompt, c_sample, cache_l0_conv, state_l0_delta, cache_l0_fox_k, cache_l0_fox_v, cache_l0_fox_logf, state_l1_shift, state_l1_wkv, l0_ada_w, l0_ada_b, l0_norm_mix, l0_norm_ff, l0_w_in, l0_conv_w, l0_a_log, l0_dt_bias, l0_gdn_onorm, l0_fox_qnorm, l0_fox_knorm, l0_fox_fbias, l0_w_out, l0_ff_w1, l0_ff_w2, l1_ada_w, l1_ada_b, l1_norm_mix, l1_norm_ff, l1_mu, l1_w_r, l1_w_k, l1_w_v, l1_w0, l1_w1, l1_w2, l1_a0, l1_a1, l1_a2, l1_g1, l1_g2, l1_k_k, l1_k_a, l1_r_k, l1_ln_w, l1_ln_b, l1_w_o, l1_ff_w1, l1_ff_w2):
    bp, lp, d = x_prompt.shape
    bs, ls, _ = x_sample.shape
    c_all = _pad_rows(jnp.concatenate([c_prompt, c_sample], axis=0), 2 * SUBLANES)
    m0 = _ada(c_all, l0_ada_w, l0_ada_b)
    m1 = _ada(c_all, l1_ada_w, l1_ada_b)
    mods0_p, mods0_s = _mods(m0[:bp]), _mods(m0[bp:bp + bs])
    mods1_p, mods1_s = _mods(m1[:bp]), _mods(m1[bp:bp + bs])

    w0 = _layer0_weights(l0_w_in, l0_conv_w, l0_a_log, l0_dt_bias, l0_fox_fbias, l0_w_out, l0_ff_w1, l0_ff_w2)
    w1 = _layer1_weights(l1_mu, l1_w_r, l1_w_k, l1_w_v, l1_w0, l1_w1, l1_w2, l1_a0, l1_a1, l1_a2, l1_g1, l1_g2,
                         l1_k_k, l1_k_a, l1_r_k, l1_ln_w, l1_ln_b, l1_w_o, l1_ff_w1, l1_ff_w2)
    norms0 = (l0_norm_mix, l0_norm_ff, l0_gdn_onorm, l0_fox_qnorm, l0_fox_knorm)

    tm_p = min(512, lp)
    y_p, conv_p, delta_p, fox_k_p, fox_v_p, fox_logf_p = _hybrid_layer(
        x_prompt, mods0_p, jnp.zeros((bp, GDN_CONV - 1, GDN_QKV), F32),
        jnp.zeros((bp, GDN_HEADS, GDN_DK, GDN_DV), F32), None, w0, *norms0,
        tm=tm_p, chunk=min(GDN_CHUNK, lp), tb=min(256, lp), blk=min(512, lp))
    y_s, conv_s, delta_s, fox_k_s, fox_v_s, fox_logf_s = _hybrid_layer(
        x_sample, mods0_s, cache_l0_conv, state_l0_delta,
        (cache_l0_fox_k, cache_l0_fox_v, cache_l0_fox_logf), w0, *norms0,
        tm=ls, chunk=min(GDN_CHUNK, ls), tb=ls, blk=None)

    y_p, shift_p, wkv_p = _rwkv_layer(
        y_p, mods1_p, jnp.zeros((bp, d), F32), jnp.zeros((bp, RWKV_HEADS, RWKV_HEAD, RWKV_HEAD), F32),
        w1, l1_norm_mix, l1_norm_ff, tm=tm_p, chunk=min(RWKV_CHUNK, lp), tb=min(512, lp), pairs=WKV_PAIRS)
    y_s, shift_s, wkv_s = _rwkv_layer(
        y_s, mods1_s, state_l1_shift, state_l1_wkv, w1, l1_norm_mix, l1_norm_ff,
        tm=ls, chunk=min(RWKV_CHUNK, ls), tb=ls, pairs=d // LANES)

    return (y_p, y_s, conv_p, conv_s, delta_p, delta_s, fox_k_p, fox_k_s,
            fox_v_p, fox_v_s, fox_logf_p, fox_logf_s, shift_p, shift_s, wkv_p, wkv_s)
```

```python
import functools
import math

import jax
import jax.numpy as jnp
from jax import lax
from jax.experimental import pallas as pl
from jax.experimental.pallas import tpu as pltpu

F32 = jnp.float32
BF16 = jnp.bfloat16

D_MODEL = 1024
D_FF = 4 * D_MODEL
NORM_EPS = 1e-6
LOG2E = 1.4426950408889634

GDN_HEADS = 4
GDN_DK = 128
GDN_DV = 128
GDN_CONV = 4
GDN_QKV = GDN_HEADS * (2 * GDN_DK + GDN_DV)
GDN_CHUNK = 64
SOLVE_HI_LEVELS = 3

FOX_HEADS = 4
FOX_DH = 128
FOX_W = FOX_HEADS * FOX_DH

RWKV_HEAD = 64
RWKV_HEADS = D_MODEL // RWKV_HEAD
RWKV_GN_EPS = 1e-5 * RWKV_HEAD
RWKV_CHUNK = 64
WKV_PAIRS = 4
WKV_SUB = 256

LANES = 128
SUBLANES = 8
VMEM_LIMIT = 56 * 1024 * 1024

NN = (((1,), (0,)), ((), ()))
NT = (((1,), (1,)), ((), ()))
TN = (((0,), (0,)), ((), ()))


def _dot(a, b, dims=NN):
    return lax.dot_general(a, b, dims, preferred_element_type=F32)


def _mm(a, b, dims=NN):
    return _dot(a.astype(BF16), b.astype(BF16), dims)


def _split2(x):
    hi = x.astype(BF16)
    lo = (x - hi.astype(F32)).astype(BF16)
    return hi, lo


def _split3(x):
    hi = x.astype(BF16)
    r = x - hi.astype(F32)
    mid = r.astype(BF16)
    lo = (r - mid.astype(F32)).astype(BF16)
    return hi, mid, lo


def _mm3(a, b, dims=NN):
    ah, al = _split2(a)
    bh, bl = _split2(b)
    return _dot(ah, bh, dims) + (_dot(ah, bl, dims) + _dot(al, bh, dims))


def _mask_mm(mask01, x):
    x1, x2, x3 = _split3(x)
    return _dot(mask01, x1) + (_dot(mask01, x2) + _dot(mask01, x3))


def _mask_mm2(mask01, x):
    x1, x2 = _split2(x)
    return _dot(mask01, x1) + _dot(mask01, x2)


def _sigmoid(x):
    return 1.0 / (1.0 + jnp.exp(-x))


def _softplus(x):
    return jnp.maximum(x, 0.0) + jnp.log(1.0 + jnp.exp(-jnp.abs(x)))


def _silu(x):
    return x * _sigmoid(x)


def _rms_mod(x, g, sc, sh):
    ms = jnp.mean(x * x, axis=-1, keepdims=True)
    return (x * lax.rsqrt(ms + NORM_EPS) * g) * (1.0 + sc) + sh


def _tri_masks(c):
    row = lax.broadcasted_iota(jnp.int32, (c, c), 0)
    col = lax.broadcasted_iota(jnp.int32, (c, c), 1)
    return row >= col, row > col, row == col


def _cparams(sem):
    return pltpu.CompilerParams(dimension_semantics=sem, vmem_limit_bytes=VMEM_LIMIT)


def _const_spec(shape):
    nd = len(shape)
    return pl.BlockSpec(shape, lambda *_: (0,) * nd, pipeline_mode=pl.Buffered(1))


def _ada_body(c_ref, w_ref, b_ref, o_ref):
    o_ref[...] = _mm3(_silu(c_ref[...]), w_ref[...]) + b_ref[...]


def _ada(c_all, w, b):
    rows, d = c_all.shape
    n = w.shape[1]
    tn = 512
    return pl.pallas_call(
        _ada_body,
        grid=(n // tn,),
        in_specs=[pl.BlockSpec((rows, d), lambda j: (0, 0)),
                  pl.BlockSpec((d, tn), lambda j: (0, j)),
                  pl.BlockSpec((1, tn), lambda j: (0, j))],
        out_specs=pl.BlockSpec((rows, tn), lambda j: (0, j)),
        out_shape=jax.ShapeDtypeStruct((rows, n), F32),
        compiler_params=_cparams(("arbitrary",)),
        name="ada_ln",
    )(c_all, w, b.reshape(1, n))


_IN0_QKV = (0, GDN_QKV)
_IN0_Z = (GDN_QKV, GDN_QKV + 512)
_IN0_Q = (2048, 2560)
_IN0_K = (2560, 3072)
_IN0_V = (3072, 3584)
_IN0_S = (3584, 3712)
_IN0_COLS = 3712


def _inproj0_body(x_ref, sh_ref, sc_ref, g_ref, w_ref, qn_ref, kn_ref, aux_ref,
                  qkv_ref, z_ref, small_ref, qb_ref, kf_ref, vf_ref, kb_ref, vb_ref, ka_ref, cref_ref, carry):
    h = _rms_mod(x_ref[...], g_ref[...], sc_ref[...], sh_ref[...]).astype(BF16)
    proj = lambda c0, c1: _dot(h, w_ref[:, c0:c1])
    third = GDN_QKV // 3
    raw = proj(*_IN0_S)
    q = proj(*_IN0_Q)
    k = proj(*_IN0_K)
    qkv_ref[:, 0:third] = proj(_IN0_QKV[0], _IN0_QKV[0] + third)

    aux = aux_ref[...]
    a_log, dt_bias, f_bias = aux[0:1], aux[1:2], aux[2:3]
    lane = lax.broadcasted_iota(jnp.int32, raw.shape, 1)
    beta = _sigmoid(raw)
    gdec = -jnp.exp(a_log) * _softplus(raw + dt_bias)
    logf = -_softplus(-(raw + f_bias))
    small_ref[...] = jnp.where(lane < 4, beta, jnp.where(lane < 8, gdec, jnp.where(lane < 12, logf, 0.0)))

    @pl.when(pl.program_id(1) == 0)
    def _():
        carry[...] = jnp.zeros_like(carry)

    tm = raw.shape[0]
    tri, _, _ = _tri_masks(tm)
    within = _mask_mm(tri.astype(BF16), logf * LOG2E)
    cref_ref[...] = carry[...] + within[0:1, :]
    carry[...] = carry[...] + within[tm - 1:tm, :]
    bias = within[0:1, :] - within
    lane_h = lax.broadcasted_iota(jnp.int32, (tm, FOX_DH), 1)
    for hd in range(FOX_HEADS):
        col = bias[:, 8 + hd:9 + hd]
        b1 = col.astype(BF16).astype(F32)
        b2 = (col - b1).astype(BF16).astype(F32)
        b3 = col - b1 - b2
        cols = jnp.where(lane_h == 0, b1, jnp.where(lane_h == 1, b2, jnp.where(lane_h == 2, b3, 0.0)))
        ka_ref[:, hd * FOX_DH:(hd + 1) * FOX_DH] = cols.astype(BF16)

    qkv_ref[:, third:2 * third] = proj(_IN0_QKV[0] + third, _IN0_QKV[0] + 2 * third)
    qn = qn_ref[...]
    for hd in range(FOX_HEADS):
        sl = slice(hd * FOX_DH, (hd + 1) * FOX_DH)
        qh = q[:, sl]
        qh = qh * lax.rsqrt(jnp.mean(qh * qh, axis=-1, keepdims=True) + NORM_EPS) * qn * (FOX_DH ** -0.5 * LOG2E)
        qb_ref[:, sl] = qh.astype(BF16)

    qkv_ref[:, 2 * third:] = proj(_IN0_QKV[0] + 2 * third, _IN0_QKV[1])
    kn = kn_ref[...]
    for hd in range(FOX_HEADS):
        sl = slice(hd * FOX_DH, (hd + 1) * FOX_DH)
        kh = k[:, sl]
        kh = kh * lax.rsqrt(jnp.mean(kh * kh, axis=-1, keepdims=True) + NORM_EPS) * kn
        kf_ref[:, hd, :] = kh
        kb_ref[:, sl] = kh.astype(BF16)

    z_ref[...] = proj(*_IN0_Z)
    v = proj(*_IN0_V)
    for hd in range(FOX_HEADS):
        vf_ref[:, hd, :] = v[:, hd * FOX_DH:(hd + 1) * FOX_DH]
    vb_ref[...] = v.astype(BF16)


def _inproj0(x, sh, sc, g, w_all, qn, kn, aux, tm):
    b, l, d = x.shape
    if b > 1 and tm == l:
        (xf,), (shf, scf) = _flatten_streams([x], [sh, sc])
        outs = _inproj0(xf, shf, scf, g, w_all, qn, kn, aux, b * l)
        return [o.reshape((b, l) + o.shape[2:]) for o in outs[:-1]] + [outs[-1]]
    row = lambda width: pl.BlockSpec((None, tm, width), lambda bi, i: (bi, i, 0))
    mod = _mod_spec(sh, tm)
    out = lambda width, dt: jax.ShapeDtypeStruct((b, l, width), dt)
    heads4 = pl.BlockSpec((None, tm, FOX_HEADS, FOX_DH), lambda bi, i: (bi, i, 0, 0))
    heads4_shape = jax.ShapeDtypeStruct((b, l, FOX_HEADS, FOX_DH), F32)
    return pl.pallas_call(
        _inproj0_body,
        grid=(b, l // tm),
        in_specs=[row(d), mod, mod, _const_spec((1, d)), _const_spec((d, _IN0_COLS)),
                  _const_spec((1, FOX_DH)), _const_spec((1, FOX_DH)), _const_spec((SUBLANES, LANES))],
        out_specs=[row(GDN_QKV), row(512), row(LANES), row(FOX_W), heads4, heads4, row(FOX_W), row(FOX_W),
                   row(FOX_W), pl.BlockSpec((None, None, 1, LANES), lambda bi, i: (bi, i, 0, 0))],
        out_shape=[out(GDN_QKV, F32), out(512, F32), out(LANES, F32), out(FOX_W, BF16),
                   heads4_shape, heads4_shape, out(FOX_W, BF16), out(FOX_W, BF16), out(FOX_W, BF16),
                   jax.ShapeDtypeStruct((b, l // tm, 1, LANES), F32)],
        scratch_shapes=[pltpu.VMEM((1, LANES), F32)],
        compiler_params=_cparams(("arbitrary", "arbitrary")),
        name="l0_in_proj",
    )(x, sh, sc, g, w_all, qn, kn, aux)


def _chunk_masks(tb, chunk):
    shift = int(math.log2(chunk))
    row = lax.broadcasted_iota(jnp.int32, (tb, tb), 0)
    col = lax.broadcasted_iota(jnp.int32, (tb, tb), 1)
    same = (row >> shift) == (col >> shift)
    return same & (row >= col), same & (row > col), row == col


def _gdn_body(qkv_ref, z_ref, small_ref, cinit_ref, s0_ref, cw_ref, on_ref,
              o_ref, sout_ref, ext, state, *, chunk):
    c = chunk
    tb = qkv_ref.shape[0]
    ci = pl.program_id(1)

    @pl.when(ci == 0)
    def _():
        ext[0:SUBLANES, :] = cinit_ref[...]
        state[...] = s0_ref[...]

    ext[SUBLANES:SUBLANES + tb, :] = qkv_ref[...]
    cw = cw_ref[...]
    base = SUBLANES - (GDN_CONV - 1)
    y = ext[base:base + tb, :] * cw[0:1]
    for i in range(1, GDN_CONV):
        y = y + ext[base + i:base + i + tb, :] * cw[i:i + 1]
    qkv = _silu(y)
    ext[0:SUBLANES, :] = ext[tb:tb + SUBLANES, :]

    tri, strict, eye = _chunk_masks(tb, c)
    sm = small_ref[...]
    gc_all = _mask_mm(tri.astype(BF16), sm)
    onorm = on_ref[...]
    eye_d = _tri_masks(GDN_DK)[2]
    heads = []
    for hd in range(GDN_HEADS):
        q = qkv[:, hd * GDN_DK:(hd + 1) * GDN_DK]
        k = qkv[:, 512 + hd * GDN_DK:512 + (hd + 1) * GDN_DK]
        v = qkv[:, 1024 + hd * GDN_DV:1024 + (hd + 1) * GDN_DV]
        q = q * lax.rsqrt(jnp.sum(q * q, axis=-1, keepdims=True) + NORM_EPS) * GDN_DK ** -0.5
        k = k * lax.rsqrt(jnp.sum(k * k, axis=-1, keepdims=True) + NORM_EPS)
        beta = sm[:, hd:hd + 1]
        gc = gc_all[:, 4 + hd:5 + hd]
        gc_row = jnp.sum(jnp.where(eye, gc, 0.0), axis=0, keepdims=True)
        dec = jnp.where(tri, jnp.exp(jnp.where(tri, gc - gc_row, 0.0)), 0.0)
        kb = k.astype(BF16)
        egc = jnp.exp(gc)
        heads.append(dict(k=k, qe=q * egc,
                          n=-jnp.where(strict, beta * _dot(kb, kb, NT) * dec, 0.0),
                          sol=jnp.concatenate([k * (beta * egc), v * beta], axis=1),
                          qk=_dot(q.astype(BF16), kb, NT) * dec))
    levels = max(1, int(math.ceil(math.log2(c))))
    for lvl in range(levels):
        mm = _mm3 if lvl < SOLVE_HI_LEVELS else _mm
        for x in heads:
            x["sol"] = x["sol"] + mm(x["n"], x["sol"])
        if lvl + 1 < levels:
            for x in heads:
                x["n"] = mm(x["n"], x["n"])
    for x in heads:
        qs = _mm(x["qk"], x["sol"])
        x["oq"] = x["qe"] - qs[:, :GDN_DK]
        x["o0"] = qs[:, GDN_DK:]
    s_hs = [state[hd] for hd in range(GDN_HEADS)]
    for c0 in range(0, tb, c):
        rows = slice(c0, c0 + c)
        for hd, x in enumerate(heads):
            g_last = gc_all[c0 + c - 1:c0 + c, 4 + hd:5 + hd]
            ks = _mm3(x["k"][rows] * jnp.exp(g_last - gc_all[rows, 4 + hd:5 + hd]), x["sol"][rows], TN)
            o = _mm3(x["oq"][rows], s_hs[hd]) + x["o0"][rows]
            s_hs[hd] = _mm3(jnp.where(eye_d, jnp.exp(g_last), 0.0) - ks[:, :GDN_DK], s_hs[hd]) + ks[:, GDN_DK:]
            zz = z_ref[rows, hd * GDN_DV:(hd + 1) * GDN_DV]
            o = o * lax.rsqrt(jnp.mean(o * o, axis=-1, keepdims=True) + NORM_EPS) * onorm
            o_ref[rows, hd * GDN_DV:(hd + 1) * GDN_DV] = (o * _silu(zz)).astype(BF16)
    for hd in range(GDN_HEADS):
        state[hd] = s_hs[hd]

    @pl.when(ci == pl.num_programs(1) - 1)
    def _():
        sout_ref[...] = state[...]


def _gdn(qkv_raw, z, small, conv_init8, s0, conv_w8, onorm, chunk, tb):
    b, l, _ = qkv_raw.shape
    row = lambda width: pl.BlockSpec((None, tb, width), lambda bi, i: (bi, i, 0))
    st = pl.BlockSpec((None, GDN_HEADS, GDN_DK, GDN_DV), lambda bi, i: (bi, 0, 0, 0))
    return pl.pallas_call(
        functools.partial(_gdn_body, chunk=chunk),
        grid=(b, l // tb),
        in_specs=[row(GDN_QKV), row(512), row(LANES),
                  pl.BlockSpec((None, SUBLANES, GDN_QKV), lambda bi, i: (bi, 0, 0)), st,
                  _const_spec((SUBLANES, GDN_QKV)), _const_spec((1, GDN_DV))],
        out_specs=[row(512), st],
        out_shape=[jax.ShapeDtypeStruct((b, l, 512), BF16),
                   jax.ShapeDtypeStruct((b, GDN_HEADS, GDN_DK, GDN_DV), F32)],
        scratch_shapes=[pltpu.VMEM((tb + SUBLANES, GDN_QKV), F32),
                        pltpu.VMEM((GDN_HEADS, GDN_DK, GDN_DV), F32)],
        compiler_params=_cparams(("arbitrary", "arbitrary")),
        name="gdn_mixer",
    )(qkv_raw, z, small, conv_init8, s0, conv_w8, onorm)


def _cumsum_body(x_ref, o_ref):
    rows = x_ref.shape[1]
    li = lax.broadcasted_iota(jnp.int32, (LANES, LANES), 0)
    lj = lax.broadcasted_iota(jnp.int32, (LANES, LANES), 1)
    upper = (li <= lj).astype(BF16)
    ones = jnp.ones((LANES, LANES), BF16)
    ri = lax.broadcasted_iota(jnp.int32, (rows, rows), 0)
    rj = lax.broadcasted_iota(jnp.int32, (rows, rows), 1)
    before = (rj < ri).astype(BF16)
    for g in range(x_ref.shape[0]):
        x1, x2, x3 = _split3(x_ref[g])
        within = _dot(x1, upper) + (_dot(x2, upper) + _dot(x3, upper))
        total = _dot(x1, ones) + (_dot(x2, ones) + _dot(x3, ones))
        o_ref[g] = within + _mask_mm(before, total)


def _cumsum_rows(x):
    g, rows, _ = x.shape
    per_step = math.gcd(g, SUBLANES)
    spec = pl.BlockSpec((per_step, rows, LANES), lambda i: (i, 0, 0))
    return pl.pallas_call(
        _cumsum_body, grid=(g // per_step,), in_specs=[spec], out_specs=spec,
        out_shape=jax.ShapeDtypeStruct(x.shape, F32),
        compiler_params=_cparams(("arbitrary",)),
        name="logf_cumsum",
    )(x)


def _fox_prompt_body(cref_ref, q_ref, k_ref, ka_ref, v_ref, o_ref, s_a, s_b, m_scr, acc_scr, *, blk):
    hd = pl.program_id(0)
    qi = pl.program_id(1)
    half = blk // 2
    lane = lax.broadcasted_iota(jnp.int32, (blk, FOX_DH), 1)
    ones = jnp.ones((half, FOX_DH), BF16)
    q = jnp.concatenate([q_ref[...], jnp.where(lane < 3, 1.0, 0.0).astype(BF16)], axis=1)
    c0 = cref_ref[hd, qi]
    row = lax.broadcasted_iota(jnp.int32, (half, half), 0)
    col = lax.broadcasted_iota(jnp.int32, (half, half), 1)
    m_scr[...] = jnp.full_like(m_scr, -jnp.inf)
    acc_scr[...] = jnp.zeros_like(acc_scr)

    def sub_rows(first_col):
        return blk if first_col is None else half

    def visibility(i, first_col):
        sub = sub_rows(first_col)
        if first_col is None or first_col + half - 1 <= i * sub:
            return "all"
        return "none" if first_col > (i + 1) * sub - 1 else "causal"

    def scores(h, s_ref, first_col=None):
        rows = pl.ds(pl.multiple_of(h * half, half), half)
        kh = jnp.concatenate([k_ref[rows, :], ka_ref[rows, :]], axis=1)
        sub = sub_rows(first_col)
        for i in range(blk // sub):
            if visibility(i, first_col) != "none":
                s_ref[i * sub:(i + 1) * sub, :] = _dot(q[i * sub:(i + 1) * sub], kh, NT)

    def consume(h, s_ref, delta, first_col):
        rows = pl.ds(pl.multiple_of(h * half, half), half)
        vh = jnp.concatenate([v_ref[rows, :], ones], axis=1)
        sub = sub_rows(first_col)
        for i in range(blk // sub):
            if visibility(i, first_col) == "none":
                continue
            rs = slice(i * sub, (i + 1) * sub)
            s = s_ref[rs, :]
            if visibility(i, first_col) == "causal":
                s = jnp.where(col + first_col <= row + i * sub, s, -jnp.inf)
            m = m_scr[rs, :]
            m_new = jnp.maximum(m, jnp.max(s, axis=-1, keepdims=True) + delta)
            shift = m_new - delta
            p = jnp.concatenate([jnp.exp2(s[:, g * LANES:(g + 1) * LANES] - shift) for g in range(half // LANES)],
                                axis=1)
            pv = _dot(p.astype(BF16), vh)
            alpha = jnp.exp2(m - m_new)
            acc = acc_scr[rs, :]
            acc_scr[rs, :] = jnp.concatenate(
                [alpha * acc[:, g * LANES:(g + 1) * LANES] + pv[:, g * LANES:(g + 1) * LANES] for g in range(2)],
                axis=1)
            m_scr[rs, :] = m_new

    def past_block(j):
        delta = c0 - cref_ref[hd, j]
        scores(2 * j + 1, s_b)
        consume(2 * j, s_a, delta, None)
        scores(2 * j + 2, s_a)
        consume(2 * j + 1, s_b, delta, None)

    def two_past_blocks(jj, carry):
        past_block(2 * jj)
        past_block(2 * jj + 1)
        return carry

    scores(0, s_a)
    lax.fori_loop(0, qi // 2, two_past_blocks, 0)

    @pl.when(qi % 2 == 1)
    def _():
        past_block(qi - 1)

    scores(2 * qi + 1, s_b, half)
    consume(2 * qi, s_a, 0.0, 0)
    consume(2 * qi + 1, s_b, 0.0, half)
    acc = acc_scr[...]
    o_ref[...] = (acc[:, :FOX_DH] / acc[:, FOX_DH:FOX_DH + 1]).astype(BF16)


def _fox_prompt(q, k, ka, v, cref, blk):
    l = q.shape[0]
    nb = l // blk
    res = pl.BlockSpec((l, FOX_DH), lambda h, i: (0, h))
    return pl.pallas_call(
        functools.partial(_fox_prompt_body, blk=blk),
        grid=(FOX_HEADS, nb),
        in_specs=[pl.BlockSpec(memory_space=pltpu.SMEM),
                  pl.BlockSpec((blk, FOX_DH), lambda h, i: (i, h)), res, res, res],
        out_specs=pl.BlockSpec((blk, FOX_DH), lambda h, i: (i, h)),
        out_shape=jax.ShapeDtypeStruct((l, FOX_W), BF16),
        scratch_shapes=[pltpu.VMEM((blk, blk // 2), F32), pltpu.VMEM((blk, blk // 2), F32),
                        pltpu.VMEM((blk, LANES), F32), pltpu.VMEM((blk, 2 * FOX_DH), F32)],
        compiler_params=_cparams(("arbitrary", "arbitrary")),
        name="fox_prompt_attn",
    )(cref, q, k, ka, v)


def _fox_step_body(q_ref, kn_ref, vn_ref, kc_ref, vc_ref, cc_ref, cnr_ref, cnc_ref, o_ref):
    q = q_ref[...]
    n = q.shape[0]
    c_i = cnc_ref[...]
    s_c = _mm(q, kc_ref[...], NT) + (c_i - cc_ref[...]) * LOG2E
    s_n = _mm(q, kn_ref[...], NT) + (c_i - cnr_ref[...]) * LOG2E
    tri, _, _ = _tri_masks(n)
    s_n = jnp.where(tri, s_n, -jnp.inf)
    m = jnp.maximum(jnp.max(s_c, axis=-1, keepdims=True), jnp.max(s_n, axis=-1, keepdims=True))
    p_c = jnp.exp2(s_c - m)
    p_n = jnp.exp2(s_n - m)
    l = jnp.sum(p_c, axis=-1, keepdims=True) + jnp.sum(p_n, axis=-1, keepdims=True)
    o = _mm(p_c, vc_ref[...]) + _mm(p_n, vn_ref[...])
    o_ref[...] = (o / l).astype(BF16)


def _fox_step(q, k_new, v_new, k_cache, v_cache, c_cache, c_new_row, c_new_col):
    b, n, _ = q.shape
    p = k_cache.shape[1]
    new = pl.BlockSpec((None, n, FOX_DH), lambda bi, h: (bi, 0, h))
    cache = pl.BlockSpec((None, p, FOX_DH), lambda bi, h: (bi, 0, h))
    return pl.pallas_call(
        _fox_step_body,
        grid=(b, FOX_HEADS),
        in_specs=[new, new, new, cache, cache,
                  pl.BlockSpec((None, None, 1, p), lambda bi, h: (bi, h, 0, 0)),
                  pl.BlockSpec((None, None, 1, n), lambda bi, h: (bi, h, 0, 0)),
                  pl.BlockSpec((None, None, n, 1), lambda bi, h: (bi, h, 0, 0))],
        out_specs=new,
        out_shape=jax.ShapeDtypeStruct((b, n, FOX_W), BF16),
        compiler_params=_cparams(("arbitrary", "arbitrary")),
        name="fox_step_attn",
    )(q, k_new, v_new, k_cache, v_cache, c_cache, c_new_row, c_new_col)


def _resid_mlp(x, mix, gt1, sh2, sc2, gt2, g2, w1_ref, w2_ref):
    x1 = x + gt1 * mix
    h2 = _rms_mod(x1, g2, sc2, sh2).astype(BF16)
    acc = jnp.zeros_like(x1)
    step = D_MODEL
    for c0 in range(0, D_FF, step):
        hid = jnp.maximum(_dot(h2, w1_ref[:, c0:c0 + step]), 0.0)
        acc = acc + _dot((hid * hid).astype(BF16), w2_ref[c0:c0 + step, :])
    return x1 + gt2 * acc


def _mix0_mlp_body(x_ref, oa_ref, ob_ref, gt1, sh2, sc2, gt2, g2, wo_ref, w1_ref, w2_ref, out_ref):
    half = GDN_HEADS * GDN_DV
    mix = _dot(oa_ref[...], wo_ref[0:half, :]) + _dot(ob_ref[...], wo_ref[half:, :])
    out_ref[...] = _resid_mlp(x_ref[...], mix, gt1[...], sh2[...], sc2[...], gt2[...], g2[...], w1_ref, w2_ref)


def _mix1_mlp_body(x_ref, y_ref, gate_ref, gt1, sh2, sc2, gt2, g2, wo_ref, w1_ref, w2_ref, out_ref):
    mix = _mm(y_ref[...] * gate_ref[...], wo_ref[...])
    out_ref[...] = _resid_mlp(x_ref[...], mix, gt1[...], sh2[...], sc2[...], gt2[...], g2[...], w1_ref, w2_ref)


def _flatten_streams(arrays, mods):
    b, l, d = arrays[0].shape
    flat = [a.reshape(1, b * l, a.shape[-1]) for a in arrays]
    per_row = [jnp.broadcast_to(m, (b, l, d)).reshape(1, b * l, d) for m in mods]
    return flat, per_row


def _mod_spec(m, tm):
    d = m.shape[-1]
    if m.shape[1] == 1:
        return pl.BlockSpec((None, 1, d), lambda bi, i: (bi, 0, 0))
    return pl.BlockSpec((None, tm, d), lambda bi, i: (bi, i, 0))


def _mix_mlp(body, name, x, mix_in, mods, g2, wo, w1, w2, tm):
    b, l, d = x.shape
    if b > 1 and tm == l:
        flat, per_row = _flatten_streams([x] + mix_in, mods)
        return _mix_mlp(body, name, flat[0], flat[1:], per_row, g2, wo, w1, w2, b * l).reshape(b, l, d)
    row = lambda width: pl.BlockSpec((None, tm, width), lambda bi, i: (bi, i, 0))
    mod = _mod_spec(mods[0], tm)
    return pl.pallas_call(
        body,
        grid=(b, l // tm),
        in_specs=[row(d)] + [row(a.shape[-1]) for a in mix_in] + [mod] * 4
                 + [_const_spec((1, d)), _const_spec(wo.shape), _const_spec(w1.shape), _const_spec(w2.shape)],
        out_specs=row(d),
        out_shape=jax.ShapeDtypeStruct((b, l, d), F32),
        compiler_params=_cparams(("arbitrary", "arbitrary")),
        name=name,
    )(x, *mix_in, *mods, g2, wo, w1, w2)


def _rwkv_proj_body(x_ref, shp_ref, sh_ref, sc_ref, g_ref, mu_ref, vec_ref,
                    wr_ref, wk_ref, wv_ref, w1_ref, w2_ref, a1_ref, a2_ref, g1_ref, g2_ref,
                    r_ref, lw_ref, k_ref, v_ref, a_ref, gate_ref, shift_ref, hext):
    tm = x_ref.shape[0]
    i = pl.program_id(1)
    h = _rms_mod(x_ref[...], g_ref[...], sc_ref[...], sh_ref[...])

    @pl.when(i == 0)
    def _():
        hext[SUBLANES - 1:SUBLANES, :] = shp_ref[...]

    hext[SUBLANES:SUBLANES + tm, :] = h
    h_prev = hext[SUBLANES - 1:SUBLANES - 1 + tm, :]
    hext[SUBLANES - 1:SUBLANES, :] = h[tm - 1:tm, :]
    shift_ref[...] = h[tm - 1:tm, :]

    xx = h_prev - h
    mu = mu_ref[...]
    vec = vec_ref[...]
    w0, a0 = vec[0:1], vec[1:2]
    mixed = lambda j: (h + xx * mu[j:j + 1]).astype(BF16)
    x_r = mixed(0)
    x_k = mixed(2)
    r_ref[...] = _dot(x_r, wr_ref[...])
    x_v = mixed(3)
    k_ref[...] = _dot(x_k, wk_ref[...])
    x_w = mixed(1)
    v_ref[...] = _dot(x_v, wv_ref[...])
    x_a = mixed(4)
    t_w = _dot(x_w, w1_ref[...])
    x_g = mixed(5)
    t_a = _dot(x_a, a1_ref[...])
    t_g = _dot(x_g, g1_ref[...])
    wl = w0 + _mm(jnp.tanh(t_w), w2_ref[...])
    a_ref[...] = _sigmoid(a0 + _mm(t_a, a2_ref[...]))
    lw_ref[...] = -jnp.exp(-_softplus(-wl) - 0.5)
    gate_ref[...] = _mm(_sigmoid(t_g), g2_ref[...])


def _rwkv_proj(x, shift_prev, sh, sc, g, mu8, vec8, weights, tm):
    b, l, d = x.shape
    row = pl.BlockSpec((None, tm, d), lambda bi, i: (bi, i, 0))
    mod = pl.BlockSpec((None, 1, d), lambda bi, i: (bi, 0, 0))
    big = jax.ShapeDtypeStruct((b, l, d), F32)
    return pl.pallas_call(
        _rwkv_proj_body,
        grid=(b, l // tm),
        in_specs=[row, mod, mod, mod, _const_spec((1, d)), _const_spec((SUBLANES, d)), _const_spec((SUBLANES, d))]
                 + [_const_spec(w.shape) for w in weights],
        out_specs=[row] * 6 + [mod],
        out_shape=[big] * 6 + [jax.ShapeDtypeStruct((b, 1, d), F32)],
        scratch_shapes=[pltpu.VMEM((tm + SUBLANES, d), F32)],
        compiler_params=_cparams(("arbitrary", "arbitrary")),
        name="rwkv_proj",
    )(x, shift_prev, sh, sc, g, mu8, vec8, *weights)


def _pair_sum(x, first):
    lo = jnp.sum(jnp.where(first, x, 0.0), axis=-1, keepdims=True)
    hi = jnp.sum(jnp.where(first, 0.0, x), axis=-1, keepdims=True)
    return jnp.where(first, lo, hi)


def _wkv_body(r_ref, lw_ref, k_ref, v_ref, a_ref, vec_ref, s0_ref, y_ref, sout_ref, state, *, chunk):
    c = chunk
    tb = r_ref.shape[0]
    n = RWKV_HEAD
    t = pl.program_id(2)
    pairs = r_ref.shape[1] // LANES

    @pl.when(t == 0)
    def _():
        state[...] = jnp.zeros_like(state)
        for p in range(pairs):
            state[p, 0:n, 0:n] = s0_ref[2 * p]
            state[p, n:, n:] = s0_ref[2 * p + 1]

    ts = min(tb, WKV_SUB)
    tri, strict, eye = _chunk_masks(ts, c)
    first = lax.broadcasted_iota(jnp.int32, (ts, LANES), 1) < n
    hr = lax.broadcasted_iota(jnp.int32, (LANES, LANES), 0)
    hc = lax.broadcasted_iota(jnp.int32, (LANES, LANES), 1)
    same_head = (hr < n) == (hc < n)
    eye_l = jnp.where(hr == hc, 1.0, 0.0)

    fc = lax.broadcasted_iota(jnp.int32, (c, LANES), 1) < n
    levels = max(1, int(math.ceil(math.log2(c))))
    lanes = [slice(p * LANES, (p + 1) * LANES) for p in range(pairs)]
    subs = [slice(t0, t0 + ts) for t0 in range(0, tb, ts)]

    ctx = [_wkv_scores(r_ref[rs, ls], lw_ref[rs, ls], k_ref[rs, ls], v_ref[rs, ls], a_ref[rs, ls], vec_ref[:, ls],
                       c, tri, strict, eye, first) for rs in subs for ls in lanes]
    for _ in range(1, levels):
        for x in ctx:
            x["nks"] = [_mm(nk, nk) for nk in x["nks"]]
        for x in ctx:
            x["t_invs"] = [t_inv + _mm(nk, t_inv) for nk, t_inv in zip(x["nks"], x["t_invs"])]
    for x in ctx:
        x["ws"] = [_mm(t_inv, rhs) for t_inv, rhs in zip(x["t_invs"], x["rhs"])]
    for x in ctx:
        _wkv_state_free_terms(x, first)

    sps = [state[p] for p in range(pairs)]
    for c0 in range(0, tb, c):
        sub, off = divmod(c0, ts)
        rows = slice(off, off + c)
        for p in range(pairs):
            x = ctx[sub * pairs + p]
            gam = jnp.exp(x["cl"][off + c - 1:off + c, :])
            gp = jnp.where(same_head, eye_l + _mm(x["wa"][rows], x["bt"][rows], TN), 0.0) * gam
            hp = jnp.where(same_head,
                           _dot(jnp.concatenate([x["u0"][rows], x["v"][rows]], axis=0).astype(BF16),
                                jnp.concatenate([x["bt"][rows], x["kt"][rows]], axis=0).astype(BF16), TN), 0.0) * gam
            y = _mm(x["rq"][rows], sps[p], NT) + x["y0"][rows]
            sps[p] = _mm(sps[p], gp) + hp
            yc = y - _pair_sum(y, fc) * (1.0 / n)
            var = _pair_sum(yc * yc, fc) * (1.0 / n)
            y_ref[c0:c0 + c, lanes[p]] = (yc * lax.rsqrt(var + RWKV_GN_EPS) * x["ln_w"] + x["ln_b"]
                                          + x["bonus"][rows])
    for p in range(pairs):
        state[p] = sps[p]

    @pl.when(t == pl.num_programs(2) - 1)
    def _():
        for p in range(pairs):
            sout_ref[2 * p] = state[p, 0:n, 0:n]
            sout_ref[2 * p + 1] = state[p, n:, n:]


def _wkv_scores(r, lw, k, v, a, vec, c, tri, strict, eye, first):
    tb = r.shape[0]
    k_k, k_a, r_k, ln_w, ln_b = vec[0:1], vec[1:2], vec[2:3], vec[3:4], vec[4:5]
    kq = k * k_k
    kkn = kq * lax.rsqrt(_pair_sum(kq * kq, first) + NORM_EPS)
    kp = k * (1.0 + (a - 1.0) * k_a)
    cl = _mask_mm2(tri.astype(BF16), lw)
    e_neg = jnp.exp(-cl)
    at = -kkn * jnp.exp(cl - lw)
    rt = r * jnp.exp(cl)
    bt = (kkn * a) * e_neg
    kt = kp * e_neg
    kcat = jnp.concatenate([bt, kt], axis=0).astype(BF16)
    vb = v.astype(BF16)
    x = dict(cl=cl, rt=rt, bt=bt, kt=kt, v=v, ln_w=ln_w, ln_b=ln_b,
             bonus=_pair_sum(r * kp * r_k, first) * v, arbs=[], avrs=[], nks=[], t_invs=[], rhs=[])
    for head_lanes in (first, jnp.logical_not(first)):
        at_e = jnp.where(head_lanes, at, 0.0)
        ma = _dot(at_e.astype(BF16), kcat, NT)
        mr = _dot(jnp.where(head_lanes, rt, 0.0).astype(BF16), kcat, NT)
        nk = jnp.where(strict, ma[:, :tb], 0.0)
        av_a = _dot(jnp.where(strict, ma[:, tb:], 0.0).astype(BF16), vb)
        x["avrs"].append(_dot(jnp.where(tri, mr[:, tb:], 0.0).astype(BF16), vb))
        x["arbs"].append(jnp.where(tri, mr[:, :tb], 0.0).astype(BF16))
        x["nks"].append(nk)
        x["t_invs"].append(jnp.where(eye, 1.0, nk))
        x["rhs"].append(jnp.concatenate([at_e, av_a], axis=1))
    return x


def _wkv_state_free_terms(x, first):
    ws, arbs, avrs = x["ws"], x["arbs"], x["avrs"]
    x["wa"] = ws[0][:, :LANES] + ws[1][:, :LANES]
    x["u0"] = jnp.where(first, ws[0][:, LANES:], ws[1][:, LANES:])
    wcat = jnp.concatenate([x["wa"], x["u0"]], axis=1).astype(BF16)
    aw0 = _dot(arbs[0], wcat)
    aw1 = _dot(arbs[1], wcat)
    x["rq"] = x["rt"] + jnp.where(first, aw0[:, :LANES], aw1[:, :LANES])
    x["y0"] = jnp.where(first, aw0[:, LANES:] + avrs[0], aw1[:, LANES:] + avrs[1])


def _wkv(r, lw, k, v, a, vec8, s0, chunk, tb, pairs):
    b, l, d = r.shape
    width = pairs * LANES
    heads = width // RWKV_HEAD
    row = pl.BlockSpec((None, tb, width), lambda bi, p, t: (bi, t, p))
    st = pl.BlockSpec((None, heads, RWKV_HEAD, RWKV_HEAD), lambda bi, p, t: (bi, p, 0, 0))
    return pl.pallas_call(
        functools.partial(_wkv_body, chunk=chunk),
        grid=(b, d // width, l // tb),
        in_specs=[row] * 5 + [pl.BlockSpec((SUBLANES, width), lambda bi, p, t: (0, p)), st],
        out_specs=[row, st],
        out_shape=[jax.ShapeDtypeStruct((b, l, d), F32),
                   jax.ShapeDtypeStruct((b, RWKV_HEADS, RWKV_HEAD, RWKV_HEAD), F32)],
        scratch_shapes=[pltpu.VMEM((pairs, LANES, LANES), F32)],
        compiler_params=_cparams(("arbitrary", "arbitrary", "arbitrary")),
        name="wkv7_chunked",
    )(r, lw, k, v, a, vec8, s0)


def _pad_rows(a, rows):
    return jnp.pad(a, ((0, rows - a.shape[0]), (0, 0)))


def _mods(m_rows):
    return [t[:, None, :] for t in jnp.split(m_rows, 6, axis=-1)]


def _layer0_weights(w_in, conv_w, a_log, dt_bias, fox_fbias, w_out, ff_w1, ff_w2):
    c = (0, 1536, 2048, 2052, 2056, 2568, 3080, 3592, 3596)
    seg = lambda i: w_in[:, c[i]:c[i + 1]]
    small = jnp.concatenate([seg(2), seg(3), seg(7)], axis=1)
    small = jnp.pad(small, ((0, 0), (0, LANES - small.shape[1])))
    w_all = jnp.concatenate([seg(0), seg(1), seg(4), seg(5), seg(6), small], axis=1).astype(BF16)
    aux = jnp.zeros((SUBLANES, LANES), F32)
    aux = aux.at[0, 4:8].set(a_log).at[1, 4:8].set(dt_bias).at[2, 8:12].set(fox_fbias)
    return dict(w_all=w_all, aux=aux, conv_w8=_pad_rows(conv_w, SUBLANES),
                w_out=w_out.astype(BF16), ff_w1=ff_w1.astype(BF16), ff_w2=ff_w2.astype(BF16))


def _hybrid_layer(x, mods, conv_buf, s0, fox_cache, wts, norm_mix, norm_ff, gdn_onorm, fox_qnorm, fox_knorm,
                  tm, chunk, tb, blk):
    b, l, d = x.shape
    sh1, sc1, gt1, sh2, sc2, gt2 = mods
    row = lambda g: g.reshape(1, -1)
    qkv_raw, z, small, q_b, k_f, v_f, k_b, v_b, k_a, cref = _inproj0(
        x, sh1, sc1, row(norm_mix), wts["w_all"], row(fox_qnorm), row(fox_knorm), wts["aux"], tm)
    conv_new = qkv_raw[:, l - (GDN_CONV - 1):, :]
    logf = small[:, :, 8:12]
    conv_init8 = jnp.pad(conv_buf, ((0, 0), (SUBLANES - (GDN_CONV - 1), 0), (0, 0)))
    o_a, s_new = _gdn(qkv_raw, z, small, conv_init8, s0, wts["conv_w8"], row(gdn_onorm), chunk, tb)

    if fox_cache is None:
        assert blk == tm and b == 1
        o_b = _fox_prompt(q_b[0], k_b[0], k_a[0], v_b[0], jnp.transpose(cref[0, :, 0, 8:8 + FOX_HEADS]), blk)[None]
    else:
        k_cache, v_cache, logf_cache = fox_cache
        p = k_cache.shape[1]
        total = p + l
        padded = -(-total // (SUBLANES * LANES)) * (SUBLANES * LANES)
        lf = jnp.concatenate([jnp.transpose(logf_cache, (0, 2, 1)), jnp.transpose(logf, (0, 2, 1))], axis=2)
        lf = jnp.pad(lf, ((0, 0), (0, 0), (padded - total, 0)))
        cs = _cumsum_rows(lf.reshape(b * FOX_HEADS, padded // LANES, LANES)).reshape(b, FOX_HEADS, padded)
        c_cache = cs[:, :, None, padded - total:padded - l]
        c_new = cs[:, :, padded - l:]
        o_b = _fox_step(q_b, k_f.reshape(b, l, FOX_W), v_f.reshape(b, l, FOX_W),
                        k_cache.reshape(b, p, FOX_W), v_cache.reshape(b, p, FOX_W),
                        c_cache, c_new[:, :, None, :], c_new[:, :, :, None])

    x = _mix_mlp(_mix0_mlp_body, "l0_out_mlp", x, [o_a, o_b], [gt1, sh2, sc2, gt2], row(norm_ff),
                 wts["w_out"], wts["ff_w1"], wts["ff_w2"], tm)
    return (x, conv_new, s_new, k_f, v_f, logf)


def _layer1_weights(mu, w_r, w_k, w_v, w0, w1, w2, a0, a1, a2, g1, g2, k_k, k_a, r_k, ln_w, ln_b, w_o, ff_w1, ff_w2):
    padc = lambda w: jnp.pad(w, ((0, 0), (0, -w.shape[1] % LANES))).astype(BF16)
    padr = lambda w: jnp.pad(w, ((0, -w.shape[0] % LANES), (0, 0))).astype(BF16)
    proj = [w_r.astype(BF16), w_k.astype(BF16), w_v.astype(BF16),
            padc(w1), padr(w2), padc(a1), padr(a2), padc(g1), padr(g2)]
    return dict(mu8=_pad_rows(mu, SUBLANES), pvec=_pad_rows(jnp.stack([w0, a0]), SUBLANES),
                svec=_pad_rows(jnp.stack([k_k, k_a, r_k.reshape(-1), ln_w, ln_b]), SUBLANES),
                proj=proj, w_o=w_o.astype(BF16), ff_w1=ff_w1.astype(BF16), ff_w2=ff_w2.astype(BF16))


def _rwkv_layer(x, mods, shift_prev, s0, wts, norm_mix, norm_ff, tm, chunk, tb, pairs):
    sh1, sc1, gt1, sh2, sc2, gt2 = mods
    row = lambda g: g.reshape(1, -1)
    r, lw, k, v, a, gate, shift_new = _rwkv_proj(
        x, shift_prev[:, None, :], sh1, sc1, row(norm_mix), wts["mu8"], wts["pvec"], wts["proj"], tm)
    y, s_new = _wkv(r, lw, k, v, a, wts["svec"], s0, chunk, tb, pairs)
    x = _mix_mlp(_mix1_mlp_body, "l1_out_mlp", x, [y, gate], [gt1, sh2, sc2, gt2], row(norm_ff),
                 wts["w_o"], wts["ff_w1"], wts["ff_w2"], tm)
    return x, shift_new[:, 0, :], s_new


def kernel(x_prompt, x_sample, c_prompt, c_sample, cache_l0_conv, state_l0_delta, cache_l0_fox_k, cache_l0_fox_v, cache_l0_fox_logf, state_l1_shift, state_l1_wkv, l0_ada_w, l0_ada_b, l0_norm_mix, l0_norm_ff, l0_w_in, l0_conv_w, l0_a_log, l0_dt_bias, l0_gdn_onorm, l0_fox_qnorm, l0_fox_knorm, l0_fox_fbias, l0_w_out, l0_ff_w1, l0_ff_w2, l1_ada_w, l1_ada_b, l1_norm_mix, l1_norm_ff, l1_mu, l1_w_r, l1_w_k, l1_w_v, l1_w0, l1_w1, l1_w2, l1_a0, l1_a1, l1_a2, l1_g1, l1_g2, l1_k_k, l1_k_a, l1_r_k, l1_ln_w, l1_ln_b, l1_w_o, l1_ff_w1, l1_ff_w2):
    bp, lp, d = x_prompt.shape
    bs, ls, _ = x_sample.shape
    c_all = _pad_rows(jnp.concatenate([c_prompt, c_sample], axis=0), 2 * SUBLANES)
    m0 = _ada(c_all, l0_ada_w, l0_ada_b)
    m1 = _ada(c_all, l1_ada_w, l1_ada_b)
    mods0_p, mods0_s = _mods(m0[:bp]), _mods(m0[bp:bp + bs])
    mods1_p, mods1_s = _mods(m1[:bp]), _mods(m1[bp:bp + bs])

    w0 = _layer0_weights(l0_w_in, l0_conv_w, l0_a_log, l0_dt_bias, l0_fox_fbias, l0_w_out, l0_ff_w1, l0_ff_w2)
    w1 = _layer1_weights(l1_mu, l1_w_r, l1_w_k, l1_w_v, l1_w0, l1_w1, l1_w2, l1_a0, l1_a1, l1_a2, l1_g1, l1_g2,
                         l1_k_k, l1_k_a, l1_r_k, l1_ln_w, l1_ln_b, l1_w_o, l1_ff_w1, l1_ff_w2)
    norms0 = (l0_norm_mix, l0_norm_ff, l0_gdn_onorm, l0_fox_qnorm, l0_fox_knorm)

    tm_p = min(512, lp)
    y_p, conv_p, delta_p, fox_k_p, fox_v_p, fox_logf_p = _hybrid_layer(
        x_prompt, mods0_p, jnp.zeros((bp, GDN_CONV - 1, GDN_QKV), F32),
        jnp.zeros((bp, GDN_HEADS, GDN_DK, GDN_DV), F32), None, w0, *norms0,
        tm=tm_p, chunk=min(GDN_CHUNK, lp), tb=min(256, lp), blk=min(512, lp))
    y_s, conv_s, delta_s, fox_k_s, fox_v_s, fox_logf_s = _hybrid_layer(
        x_sample, mods0_s, cache_l0_conv, state_l0_delta,
        (cache_l0_fox_k, cache_l0_fox_v, cache_l0_fox_logf), w0, *norms0,
        tm=ls, chunk=min(GDN_CHUNK, ls), tb=ls, blk=None)

    y_p, shift_p, wkv_p = _rwkv_layer(
        y_p, mods1_p, jnp.zeros((bp, d), F32), jnp.zeros((bp, RWKV_HEADS, RWKV_HEAD, RWKV_HEAD), F32),
        w1, l1_norm_mix, l1_norm_ff, tm=tm_p, chunk=min(RWKV_CHUNK, lp), tb=min(512, lp), pairs=WKV_PAIRS)
    y_s, shift_s, wkv_s = _rwkv_layer(
        y_s, mods1_s, state_l1_shift, state_l1_wkv, w1, l1_norm_mix, l1_norm_ff,
        tm=ls, chunk=min(RWKV_CHUNK, ls), tb=ls, pairs=d // LANES)

    return (y_p, y_s, conv_p, conv_s, delta_p, delta_s, fox_k_p, fox_k_s,
            fox_v_p, fox_v_s, fox_logf_p, fox_logf_s, shift_p, shift_s, wkv_p, wkv_s)
```

```python
import functools
import math

import jax
import jax.numpy as jnp
from jax import lax
from jax.experimental import pallas as pl
from jax.experimental.pallas import tpu as pltpu

F32 = jnp.float32
BF16 = jnp.bfloat16

D_MODEL = 1024
D_FF = 4 * D_MODEL
NORM_EPS = 1e-6
LOG2E = 1.4426950408889634

GDN_HEADS = 4
GDN_DK = 128
GDN_DV = 128
GDN_CONV = 4
GDN_QKV = GDN_HEADS * (2 * GDN_DK + GDN_DV)
GDN_CHUNK = 64

FOX_HEADS = 4
FOX_DH = 128
FOX_W = FOX_HEADS * FOX_DH

RWKV_HEAD = 64
RWKV_HEADS = D_MODEL // RWKV_HEAD
RWKV_GN_EPS = 1e-5 * RWKV_HEAD
RWKV_CHUNK = 64
WKV_PAIRS = 4
WKV_SUB = 128

LANES = 128
SUBLANES = 8
VMEM_LIMIT = 56 * 1024 * 1024

NN = (((1,), (0,)), ((), ()))
NT = (((1,), (1,)), ((), ()))
TN = (((0,), (0,)), ((), ()))


def _dot(a, b, dims=NN):
    return lax.dot_general(a, b, dims, preferred_element_type=F32)


def _mm(a, b, dims=NN):
    return _dot(a.astype(BF16), b.astype(BF16), dims)


def _split2(x):
    hi = x.astype(BF16)
    lo = (x - hi.astype(F32)).astype(BF16)
    return hi, lo


def _split3(x):
    hi = x.astype(BF16)
    r = x - hi.astype(F32)
    mid = r.astype(BF16)
    lo = (r - mid.astype(F32)).astype(BF16)
    return hi, mid, lo


def _mm3(a, b, dims=NN):
    ah, al = _split2(a)
    bh, bl = _split2(b)
    return _dot(ah, bh, dims) + (_dot(ah, bl, dims) + _dot(al, bh, dims))


def _mask_mm(mask01, x):
    x1, x2, x3 = _split3(x)
    return _dot(mask01, x1) + (_dot(mask01, x2) + _dot(mask01, x3))


def _mask_mm2(mask01, x):
    x1, x2 = _split2(x)
    return _dot(mask01, x1) + _dot(mask01, x2)


def _ones_where(mask):
    return jnp.where(mask, 1.0, 0.0).astype(BF16)


def _sigmoid(x):
    return 1.0 / (1.0 + jnp.exp(-x))


def _softplus(x):
    return jnp.maximum(x, 0.0) + jnp.log(1.0 + jnp.exp(-jnp.abs(x)))


def _silu(x):
    return x * _sigmoid(x)


def _rms_mod(x, g, sc, sh):
    ms = jnp.mean(x * x, axis=-1, keepdims=True)
    return (x * lax.rsqrt(ms + NORM_EPS) * g) * (1.0 + sc) + sh


def _tri_masks(c):
    row = lax.broadcasted_iota(jnp.int32, (c, c), 0)
    col = lax.broadcasted_iota(jnp.int32, (c, c), 1)
    return row >= col, row > col, row == col


def _cparams(sem):
    return pltpu.CompilerParams(dimension_semantics=sem, vmem_limit_bytes=VMEM_LIMIT)


def _const_spec(shape):
    nd = len(shape)
    return pl.BlockSpec(shape, lambda *_: (0,) * nd, pipeline_mode=pl.Buffered(1))


def _ada_body(c_ref, w_ref, b_ref, o_ref):
    o_ref[...] = _mm3(_silu(c_ref[...]), w_ref[...]) + b_ref[...]


def _ada(c_all, w, b):
    rows, d = c_all.shape
    n = w.shape[1]
    tn = 512
    return pl.pallas_call(
        _ada_body,
        grid=(n // tn,),
        in_specs=[pl.BlockSpec((rows, d), lambda j: (0, 0)),
                  pl.BlockSpec((d, tn), lambda j: (0, j)),
                  pl.BlockSpec((1, tn), lambda j: (0, j))],
        out_specs=pl.BlockSpec((rows, tn), lambda j: (0, j)),
        out_shape=jax.ShapeDtypeStruct((rows, n), F32),
        compiler_params=_cparams(("arbitrary",)),
        name="ada_ln",
    )(c_all, w, b.reshape(1, n))


_IN0_QKV = (0, GDN_QKV)
_IN0_Z = (GDN_QKV, GDN_QKV + 512)
_IN0_Q = (2048, 2560)
_IN0_K = (2560, 3072)
_IN0_V = (3072, 3584)
_IN0_S = (3584, 3712)
_IN0_COLS = 3712


def _inproj0_body(x_ref, sh_ref, sc_ref, g_ref, w_ref, qn_ref, kn_ref, aux_ref,
                  qkv_ref, z_ref, small_ref, qb_ref, kf_ref, vf_ref, kb_ref, vb_ref, ka_ref, cref_ref, carry):
    h = _rms_mod(x_ref[...], g_ref[...], sc_ref[...], sh_ref[...]).astype(BF16)
    proj = lambda c0, c1: _dot(h, w_ref[:, c0:c1])
    third = GDN_QKV // 3
    raw = proj(*_IN0_S)
    q = proj(*_IN0_Q)
    k = proj(*_IN0_K)
    qkv_ref[:, 0:third] = proj(_IN0_QKV[0], _IN0_QKV[0] + third)

    aux = aux_ref[...]
    a_log, dt_bias, f_bias = aux[0:1], aux[1:2], aux[2:3]
    lane = lax.broadcasted_iota(jnp.int32, raw.shape, 1)
    beta = _sigmoid(raw)
    gdec = -jnp.exp(a_log) * _softplus(raw + dt_bias)
    logf = -_softplus(-(raw + f_bias))
    small_ref[...] = jnp.where(lane < 4, beta, jnp.where(lane < 8, gdec, jnp.where(lane < 12, logf, 0.0)))

    @pl.when(pl.program_id(1) == 0)
    def _():
        carry[...] = jnp.zeros_like(carry)

    tm = raw.shape[0]
    tri, _, _ = _tri_masks(tm)
    within = _mask_mm(_ones_where(tri), logf * LOG2E)
    cref_ref[...] = carry[...] + within[0:1, :]
    carry[...] = carry[...] + within[tm - 1:tm, :]
    bias = within[0:1, :] - within
    lane_h = lax.broadcasted_iota(jnp.int32, (tm, FOX_DH), 1)
    for hd in range(FOX_HEADS):
        col = bias[:, 8 + hd:9 + hd]
        b1 = col.astype(BF16).astype(F32)
        b2 = (col - b1).astype(BF16).astype(F32)
        b3 = col - b1 - b2
        cols = jnp.where(lane_h == 0, b1, jnp.where(lane_h == 1, b2, jnp.where(lane_h == 2, b3, 0.0)))
        ka_ref[:, hd * FOX_DH:(hd + 1) * FOX_DH] = cols.astype(BF16)

    qkv_ref[:, third:2 * third] = proj(_IN0_QKV[0] + third, _IN0_QKV[0] + 2 * third)
    qn = qn_ref[...]
    for hd in range(FOX_HEADS):
        sl = slice(hd * FOX_DH, (hd + 1) * FOX_DH)
        qh = q[:, sl]
        qh = qh * lax.rsqrt(jnp.mean(qh * qh, axis=-1, keepdims=True) + NORM_EPS) * qn * (FOX_DH ** -0.5 * LOG2E)
        qb_ref[:, sl] = qh.astype(BF16)

    qkv_ref[:, 2 * third:] = proj(_IN0_QKV[0] + 2 * third, _IN0_QKV[1])
    kn = kn_ref[...]
    for hd in range(FOX_HEADS):
        sl = slice(hd * FOX_DH, (hd + 1) * FOX_DH)
        kh = k[:, sl]
        kh = kh * lax.rsqrt(jnp.mean(kh * kh, axis=-1, keepdims=True) + NORM_EPS) * kn
        kf_ref[:, hd, :] = kh
        kb_ref[:, sl] = kh.astype(BF16)

    z_ref[...] = proj(*_IN0_Z)
    v = proj(*_IN0_V)
    for hd in range(FOX_HEADS):
        vf_ref[:, hd, :] = v[:, hd * FOX_DH:(hd + 1) * FOX_DH]
    vb_ref[...] = v.astype(BF16)


def _inproj0(x, sh, sc, g, w_all, qn, kn, aux, tm):
    b, l, d = x.shape
    if b > 1 and tm == l:
        (xf,), (shf, scf) = _flatten_streams([x], [sh, sc])
        outs = _inproj0(xf, shf, scf, g, w_all, qn, kn, aux, b * l)
        return [o.reshape((b, l) + o.shape[2:]) for o in outs[:-1]] + [outs[-1]]
    row = lambda width: pl.BlockSpec((None, tm, width), lambda bi, i: (bi, i, 0))
    mod = _mod_spec(sh, tm)
    out = lambda width, dt: jax.ShapeDtypeStruct((b, l, width), dt)
    heads4 = pl.BlockSpec((None, tm, FOX_HEADS, FOX_DH), lambda bi, i: (bi, i, 0, 0))
    heads4_shape = jax.ShapeDtypeStruct((b, l, FOX_HEADS, FOX_DH), F32)
    return pl.pallas_call(
        _inproj0_body,
        grid=(b, l // tm),
        in_specs=[row(d), mod, mod, _const_spec((1, d)), _const_spec((d, _IN0_COLS)),
                  _const_spec((1, FOX_DH)), _const_spec((1, FOX_DH)), _const_spec((SUBLANES, LANES))],
        out_specs=[row(GDN_QKV), row(512), row(LANES), row(FOX_W), heads4, heads4, row(FOX_W), row(FOX_W),
                   row(FOX_W), pl.BlockSpec((None, None, 1, LANES), lambda bi, i: (bi, i, 0, 0))],
        out_shape=[out(GDN_QKV, F32), out(512, F32), out(LANES, F32), out(FOX_W, BF16),
                   heads4_shape, heads4_shape, out(FOX_W, BF16), out(FOX_W, BF16), out(FOX_W, BF16),
                   jax.ShapeDtypeStruct((b, l // tm, 1, LANES), F32)],
        scratch_shapes=[pltpu.VMEM((1, LANES), F32)],
        compiler_params=_cparams(("arbitrary", "arbitrary")),
        name="l0_in_proj",
    )(x, sh, sc, g, w_all, qn, kn, aux)


def _chunk_masks(tb, chunk):
    shift = int(math.log2(chunk))
    row = lax.broadcasted_iota(jnp.int32, (tb, tb), 0)
    col = lax.broadcasted_iota(jnp.int32, (tb, tb), 1)
    same = (row >> shift) == (col >> shift)
    return same & (row >= col), same & (row > col), row == col


def _block_diag(blocks):
    if len(blocks) == 1:
        return blocks[0]
    zero = jnp.zeros_like(blocks[0])
    rows = [jnp.concatenate([b if j == i else zero for j in range(len(blocks))], axis=1) for i, b in enumerate(blocks)]
    return jnp.concatenate(rows, axis=0)


def _gdn_body(qkv_ref, z_ref, small_ref, cinit_ref, s0_ref, cw_ref, on_ref,
              o_ref, sout_ref, ext, state, *, chunk):
    c = chunk
    tb = qkv_ref.shape[0]
    ci = pl.program_id(1)

    @pl.when(ci == 0)
    def _():
        ext[0:SUBLANES, :] = cinit_ref[...]
        state[...] = s0_ref[...]

    ext[SUBLANES:SUBLANES + tb, :] = qkv_ref[...]
    cw = cw_ref[...]
    base = SUBLANES - (GDN_CONV - 1)
    y = ext[base:base + tb, :] * cw[0:1]
    for i in range(1, GDN_CONV):
        y = y + ext[base + i:base + i + tb, :] * cw[i:i + 1]
    qkv = _silu(y)
    ext[0:SUBLANES, :] = ext[tb:tb + SUBLANES, :]

    sb = min(tb, max(c, LANES))
    blocks = [slice(r0, r0 + sb) for r0 in range(0, tb, sb)]
    tri, strict, eye = _chunk_masks(sb, c)
    tri01 = _ones_where(tri)
    sm = small_ref[...]
    gc_all = jnp.concatenate([_mask_mm(tri01, sm[rs]) for rs in blocks], axis=0)
    onorm = on_ref[...]
    eye_d = _tri_masks(GDN_DK)[2]
    heads = []
    for hd in range(GDN_HEADS):
        q = qkv[:, hd * GDN_DK:(hd + 1) * GDN_DK]
        k = qkv[:, 512 + hd * GDN_DK:512 + (hd + 1) * GDN_DK]
        v = qkv[:, 1024 + hd * GDN_DV:1024 + (hd + 1) * GDN_DV]
        q = q * lax.rsqrt(jnp.sum(q * q, axis=-1, keepdims=True) + NORM_EPS) * GDN_DK ** -0.5
        k = k * lax.rsqrt(jnp.sum(k * k, axis=-1, keepdims=True) + NORM_EPS)
        beta = sm[:, hd:hd + 1]
        kb = k.astype(BF16)
        qb = q.astype(BF16)
        egc = jnp.exp(gc_all[:, 4 + hd:5 + hd])
        n_blocks, qk_blocks = [], []
        for rs in blocks:
            gc = gc_all[rs, 4 + hd:5 + hd]
            gc_row = jnp.sum(jnp.where(eye, gc, 0.0), axis=0, keepdims=True)
            dec = jnp.where(tri, jnp.exp(jnp.where(tri, gc - gc_row, 0.0)), 0.0)
            n_blocks.append(-jnp.where(strict, sm[rs, hd:hd + 1] * _dot(kb[rs], kb[rs], NT) * dec, 0.0))
            qk_blocks.append(_dot(qb[rs], kb[rs], NT) * dec)
        heads.append(dict(k=k, qe=q * egc, n=_block_diag(n_blocks), qk=_block_diag(qk_blocks),
                          sol=jnp.concatenate([k * (beta * egc), v * beta], axis=1)))
    levels = max(1, int(math.ceil(math.log2(c))))
    for lvl in range(levels):
        for x in heads:
            x["sol"] = x["sol"] + _mm3(x["n"], x["sol"])
        if lvl + 1 < levels:
            for x in heads:
                x["n"] = _mm3(x["n"], x["n"])
    for x in heads:
        qs = _mm(x["qk"], x["sol"])
        x["oq"] = x["qe"] - qs[:, :GDN_DK]
        x["o0"] = qs[:, GDN_DK:]
    s_hs = [state[hd] for hd in range(GDN_HEADS)]
    for c0 in range(0, tb, c):
        rows = slice(c0, c0 + c)
        for hd, x in enumerate(heads):
            g_last = gc_all[c0 + c - 1:c0 + c, 4 + hd:5 + hd]
            ks = _mm3(x["k"][rows] * jnp.exp(g_last - gc_all[rows, 4 + hd:5 + hd]), x["sol"][rows], TN)
            o = _mm3(x["oq"][rows], s_hs[hd]) + x["o0"][rows]
            s_hs[hd] = _mm3(jnp.where(eye_d, jnp.exp(g_last), 0.0) - ks[:, :GDN_DK], s_hs[hd]) + ks[:, GDN_DK:]
            zz = z_ref[rows, hd * GDN_DV:(hd + 1) * GDN_DV]
            o = o * lax.rsqrt(jnp.mean(o * o, axis=-1, keepdims=True) + NORM_EPS) * onorm
            o_ref[rows, hd * GDN_DV:(hd + 1) * GDN_DV] = (o * _silu(zz)).astype(BF16)
    for hd in range(GDN_HEADS):
        state[hd] = s_hs[hd]

    @pl.when(ci == pl.num_programs(1) - 1)
    def _():
        sout_ref[...] = state[...]


def _gdn(qkv_raw, z, small, conv_init8, s0, conv_w8, onorm, chunk, tb):
    b, l, _ = qkv_raw.shape
    row = lambda width: pl.BlockSpec((None, tb, width), lambda bi, i: (bi, i, 0))
    st = pl.BlockSpec((None, GDN_HEADS, GDN_DK, GDN_DV), lambda bi, i: (bi, 0, 0, 0))
    return pl.pallas_call(
        functools.partial(_gdn_body, chunk=chunk),
        grid=(b, l // tb),
        in_specs=[row(GDN_QKV), row(512), row(LANES),
                  pl.BlockSpec((None, SUBLANES, GDN_QKV), lambda bi, i: (bi, 0, 0)), st,
                  _const_spec((SUBLANES, GDN_QKV)), _const_spec((1, GDN_DV))],
        out_specs=[row(512), st],
        out_shape=[jax.ShapeDtypeStruct((b, l, 512), BF16),
                   jax.ShapeDtypeStruct((b, GDN_HEADS, GDN_DK, GDN_DV), F32)],
        scratch_shapes=[pltpu.VMEM((tb + SUBLANES, GDN_QKV), F32),
                        pltpu.VMEM((GDN_HEADS, GDN_DK, GDN_DV), F32)],
        compiler_params=_cparams(("arbitrary", "arbitrary")),
        name="gdn_mixer",
    )(qkv_raw, z, small, conv_init8, s0, conv_w8, onorm)


def _cumsum_body(x_ref, o_ref):
    rows = x_ref.shape[1]
    li = lax.broadcasted_iota(jnp.int32, (LANES, LANES), 0)
    lj = lax.broadcasted_iota(jnp.int32, (LANES, LANES), 1)
    upper = _ones_where(li <= lj)
    ones = jnp.ones((LANES, LANES), BF16)
    ri = lax.broadcasted_iota(jnp.int32, (rows, rows), 0)
    rj = lax.broadcasted_iota(jnp.int32, (rows, rows), 1)
    before = _ones_where(rj < ri)
    for g in range(x_ref.shape[0]):
        x1, x2, x3 = _split3(x_ref[g])
        within = _dot(x1, upper) + (_dot(x2, upper) + _dot(x3, upper))
        total = _dot(x1, ones) + (_dot(x2, ones) + _dot(x3, ones))
        o_ref[g] = within + _mask_mm(before, total)


def _cumsum_rows(x):
    g, rows, _ = x.shape
    per_step = math.gcd(g, SUBLANES)
    spec = pl.BlockSpec((per_step, rows, LANES), lambda i: (i, 0, 0))
    return pl.pallas_call(
        _cumsum_body, grid=(g // per_step,), in_specs=[spec], out_specs=spec,
        out_shape=jax.ShapeDtypeStruct(x.shape, F32),
        compiler_params=_cparams(("arbitrary",)),
        name="logf_cumsum",
    )(x)


def _fox_prompt_body(cref_ref, q_ref, k_ref, ka_ref, v_ref, o_ref, s_a, s_b, m_scr, acc_scr, *, blk):
    hd = pl.program_id(0)
    qi = pl.program_id(1)
    half = blk // 2
    lane = lax.broadcasted_iota(jnp.int32, (blk, FOX_DH), 1)
    ones = jnp.ones((half, FOX_DH), BF16)
    q = jnp.concatenate([q_ref[...], jnp.where(lane < 3, 1.0, 0.0).astype(BF16)], axis=1)
    c0 = cref_ref[hd, qi]
    row = lax.broadcasted_iota(jnp.int32, (half, half), 0)
    col = lax.broadcasted_iota(jnp.int32, (half, half), 1)
    m_scr[...] = jnp.full_like(m_scr, -jnp.inf)
    acc_scr[...] = jnp.zeros_like(acc_scr)

    def sub_rows(first_col):
        return blk if first_col is None else half

    def visibility(i, first_col):
        sub = sub_rows(first_col)
        if first_col is None or first_col + half - 1 <= i * sub:
            return "all"
        return "none" if first_col > (i + 1) * sub - 1 else "causal"

    def scores(h, s_ref, first_col=None):
        rows = pl.ds(pl.multiple_of(h * half, half), half)
        kh = jnp.concatenate([k_ref[rows, :], ka_ref[rows, :]], axis=1)
        sub = sub_rows(first_col)
        for i in range(blk // sub):
            if visibility(i, first_col) != "none":
                s_ref[i * sub:(i + 1) * sub, :] = _dot(q[i * sub:(i + 1) * sub], kh, NT)

    def consume(h, s_ref, delta, first_col):
        rows = pl.ds(pl.multiple_of(h * half, half), half)
        vh = jnp.concatenate([v_ref[rows, :], ones], axis=1)
        sub = sub_rows(first_col)
        for i in range(blk // sub):
            if visibility(i, first_col) == "none":
                continue
            rs = slice(i * sub, (i + 1) * sub)
            s = s_ref[rs, :]
            if visibility(i, first_col) == "causal":
                s = jnp.where(col + first_col <= row + i * sub, s, -jnp.inf)
            m = m_scr[rs, :]
            m_new = jnp.maximum(m, jnp.max(s, axis=-1, keepdims=True) + delta)
            shift = m_new - delta
            p = jnp.concatenate([jnp.exp2(s[:, g * LANES:(g + 1) * LANES] - shift) for g in range(half // LANES)],
                                axis=1)
            pv = _dot(p.astype(BF16), vh)
            alpha = jnp.exp2(m - m_new)
            acc = acc_scr[rs, :]
            acc_scr[rs, :] = jnp.concatenate(
                [alpha * acc[:, g * LANES:(g + 1) * LANES] + pv[:, g * LANES:(g + 1) * LANES] for g in range(2)],
                axis=1)
            m_scr[rs, :] = m_new

    def past_block(j):
        delta = c0 - cref_ref[hd, j]
        scores(2 * j + 1, s_b)
        consume(2 * j, s_a, delta, None)
        scores(2 * j + 2, s_a)
        consume(2 * j + 1, s_b, delta, None)

    def two_past_blocks(jj, carry):
        past_block(2 * jj)
        past_block(2 * jj + 1)
        return carry

    scores(0, s_a)
    lax.fori_loop(0, qi // 2, two_past_blocks, 0)

    @pl.when(qi % 2 == 1)
    def _():
        past_block(qi - 1)

    scores(2 * qi + 1, s_b, half)
    consume(2 * qi, s_a, 0.0, 0)
    consume(2 * qi + 1, s_b, 0.0, half)
    acc = acc_scr[...]
    o_ref[...] = (acc[:, :FOX_DH] / acc[:, FOX_DH:FOX_DH + 1]).astype(BF16)


def _fox_prompt(q, k, ka, v, cref, blk):
    l = q.shape[0]
    nb = l // blk
    res = pl.BlockSpec((l, FOX_DH), lambda h, i: (0, h))
    return pl.pallas_call(
        functools.partial(_fox_prompt_body, blk=blk),
        grid=(FOX_HEADS, nb),
        in_specs=[pl.BlockSpec(memory_space=pltpu.SMEM),
                  pl.BlockSpec((blk, FOX_DH), lambda h, i: (i, h)), res, res, res],
        out_specs=pl.BlockSpec((blk, FOX_DH), lambda h, i: (i, h)),
        out_shape=jax.ShapeDtypeStruct((l, FOX_W), BF16),
        scratch_shapes=[pltpu.VMEM((blk, blk // 2), F32), pltpu.VMEM((blk, blk // 2), F32),
                        pltpu.VMEM((blk, LANES), F32), pltpu.VMEM((blk, 2 * FOX_DH), F32)],
        compiler_params=_cparams(("arbitrary", "arbitrary")),
        name="fox_prompt_attn",
    )(cref, q, k, ka, v)


def _fox_step_body(q_ref, kn_ref, vn_ref, kc_ref, vc_ref, cc_ref, cnr_ref, cnc_ref, o_ref):
    q = q_ref[...]
    n = q.shape[0]
    c_i = cnc_ref[...]
    s_c = _mm(q, kc_ref[...], NT) + (c_i - cc_ref[...]) * LOG2E
    s_n = _mm(q, kn_ref[...], NT) + (c_i - cnr_ref[...]) * LOG2E
    tri, _, _ = _tri_masks(n)
    s_n = jnp.where(tri, s_n, -jnp.inf)
    m = jnp.maximum(jnp.max(s_c, axis=-1, keepdims=True), jnp.max(s_n, axis=-1, keepdims=True))
    p_c = jnp.exp2(s_c - m)
    p_n = jnp.exp2(s_n - m)
    l = jnp.sum(p_c, axis=-1, keepdims=True) + jnp.sum(p_n, axis=-1, keepdims=True)
    o = _mm(p_c, vc_ref[...]) + _mm(p_n, vn_ref[...])
    o_ref[...] = (o / l).astype(BF16)


def _fox_step(q, k_new, v_new, k_cache, v_cache, c_cache, c_new_row, c_new_col):
    b, n, _ = q.shape
    p = k_cache.shape[1]
    new = pl.BlockSpec((None, n, FOX_DH), lambda bi, h: (bi, 0, h))
    cache = pl.BlockSpec((None, p, FOX_DH), lambda bi, h: (bi, 0, h))
    return pl.pallas_call(
        _fox_step_body,
        grid=(b, FOX_HEADS),
        in_specs=[new, new, new, cache, cache,
                  pl.BlockSpec((None, None, 1, p), lambda bi, h: (bi, h, 0, 0)),
                  pl.BlockSpec((None, None, 1, n), lambda bi, h: (bi, h, 0, 0)),
                  pl.BlockSpec((None, None, n, 1), lambda bi, h: (bi, h, 0, 0))],
        out_specs=new,
        out_shape=jax.ShapeDtypeStruct((b, n, FOX_W), BF16),
        compiler_params=_cparams(("arbitrary", "arbitrary")),
        name="fox_step_attn",
    )(q, k_new, v_new, k_cache, v_cache, c_cache, c_new_row, c_new_col)


def _resid_mlp(x, mix, gt1, sh2, sc2, gt2, g2, w1_ref, w2_ref):
    x1 = x + gt1 * mix
    h2 = _rms_mod(x1, g2, sc2, sh2).astype(BF16)
    acc = jnp.zeros_like(x1)
    step = D_MODEL
    for c0 in range(0, D_FF, step):
        hid = jnp.maximum(_dot(h2, w1_ref[:, c0:c0 + step]), 0.0)
        acc = acc + _dot((hid * hid).astype(BF16), w2_ref[c0:c0 + step, :])
    return x1 + gt2 * acc


def _mix0_mlp_body(x_ref, oa_ref, ob_ref, gt1, sh2, sc2, gt2, g2, wo_ref, w1_ref, w2_ref, out_ref):
    half = GDN_HEADS * GDN_DV
    mix = _dot(oa_ref[...], wo_ref[0:half, :]) + _dot(ob_ref[...], wo_ref[half:, :])
    out_ref[...] = _resid_mlp(x_ref[...], mix, gt1[...], sh2[...], sc2[...], gt2[...], g2[...], w1_ref, w2_ref)


def _mix1_mlp_body(x_ref, y_ref, gate_ref, gt1, sh2, sc2, gt2, g2, wo_ref, w1_ref, w2_ref, out_ref):
    mix = _mm(y_ref[...] * gate_ref[...], wo_ref[...])
    out_ref[...] = _resid_mlp(x_ref[...], mix, gt1[...], sh2[...], sc2[...], gt2[...], g2[...], w1_ref, w2_ref)


def _flatten_streams(arrays, mods):
    b, l, d = arrays[0].shape
    flat = [a.reshape(1, b * l, a.shape[-1]) for a in arrays]
    per_row = [jnp.broadcast_to(m, (b, l, d)).reshape(1, b * l, d) for m in mods]
    return flat, per_row


def _mod_spec(m, tm):
    d = m.shape[-1]
    if m.shape[1] == 1:
        return pl.BlockSpec((None, 1, d), lambda bi, i: (bi, 0, 0))
    return pl.BlockSpec((None, tm, d), lambda bi, i: (bi, i, 0))


def _mix_mlp(body, name, x, mix_in, mods, g2, wo, w1, w2, tm):
    b, l, d = x.shape
    if b > 1 and tm == l:
        flat, per_row = _flatten_streams([x] + mix_in, mods)
        return _mix_mlp(body, name, flat[0], flat[1:], per_row, g2, wo, w1, w2, b * l).reshape(b, l, d)
    row = lambda width: pl.BlockSpec((None, tm, width), lambda bi, i: (bi, i, 0))
    mod = _mod_spec(mods[0], tm)
    return pl.pallas_call(
        body,
        grid=(b, l // tm),
        in_specs=[row(d)] + [row(a.shape[-1]) for a in mix_in] + [mod] * 4
                 + [_const_spec((1, d)), _const_spec(wo.shape), _const_spec(w1.shape), _const_spec(w2.shape)],
        out_specs=row(d),
        out_shape=jax.ShapeDtypeStruct((b, l, d), F32),
        compiler_params=_cparams(("arbitrary", "arbitrary")),
        name=name,
    )(x, *mix_in, *mods, g2, wo, w1, w2)


def _rwkv_proj_body(x_ref, shp_ref, sh_ref, sc_ref, g_ref, mu_ref, vec_ref,
                    wr_ref, wk_ref, wv_ref, w1_ref, w2_ref, a1_ref, a2_ref, g1_ref, g2_ref,
                    r_ref, lw_ref, k_ref, v_ref, a_ref, gate_ref, shift_ref, hext):
    tm = x_ref.shape[0]
    i = pl.program_id(1)
    h = _rms_mod(x_ref[...], g_ref[...], sc_ref[...], sh_ref[...])

    @pl.when(i == 0)
    def _():
        hext[SUBLANES - 1:SUBLANES, :] = shp_ref[...]

    hext[SUBLANES:SUBLANES + tm, :] = h
    h_prev = hext[SUBLANES - 1:SUBLANES - 1 + tm, :]
    hext[SUBLANES - 1:SUBLANES, :] = h[tm - 1:tm, :]
    shift_ref[...] = h[tm - 1:tm, :]

    xx = h_prev - h
    mu = mu_ref[...]
    vec = vec_ref[...]
    w0, a0 = vec[0:1], vec[1:2]
    mixed = lambda j: (h + xx * mu[j:j + 1]).astype(BF16)
    x_r = mixed(0)
    x_k = mixed(2)
    r_ref[...] = _dot(x_r, wr_ref[...])
    x_v = mixed(3)
    k_ref[...] = _dot(x_k, wk_ref[...])
    x_w = mixed(1)
    v_ref[...] = _dot(x_v, wv_ref[...])
    x_a = mixed(4)
    t_w = _dot(x_w, w1_ref[...])
    x_g = mixed(5)
    t_a = _dot(x_a, a1_ref[...])
    t_g = _dot(x_g, g1_ref[...])
    wl = w0 + _mm(jnp.tanh(t_w), w2_ref[...])
    a_ref[...] = _sigmoid(a0 + _mm(t_a, a2_ref[...]))
    lw_ref[...] = -jnp.exp(-_softplus(-wl) - 0.5)
    gate_ref[...] = _mm(_sigmoid(t_g), g2_ref[...])


def _rwkv_proj(x, shift_prev, sh, sc, g, mu8, vec8, weights, tm):
    b, l, d = x.shape
    row = pl.BlockSpec((None, tm, d), lambda bi, i: (bi, i, 0))
    mod = pl.BlockSpec((None, 1, d), lambda bi, i: (bi, 0, 0))
    big = jax.ShapeDtypeStruct((b, l, d), F32)
    return pl.pallas_call(
        _rwkv_proj_body,
        grid=(b, l // tm),
        in_specs=[row, mod, mod, mod, _const_spec((1, d)), _const_spec((SUBLANES, d)), _const_spec((SUBLANES, d))]
                 + [_const_spec(w.shape) for w in weights],
        out_specs=[row] * 6 + [mod],
        out_shape=[big] * 6 + [jax.ShapeDtypeStruct((b, 1, d), F32)],
        scratch_shapes=[pltpu.VMEM((tm + SUBLANES, d), F32)],
        compiler_params=_cparams(("arbitrary", "arbitrary")),
        name="rwkv_proj",
    )(x, shift_prev, sh, sc, g, mu8, vec8, *weights)


def _pair_sum(x, first):
    lo = jnp.sum(jnp.where(first, x, 0.0), axis=-1, keepdims=True)
    hi = jnp.sum(jnp.where(first, 0.0, x), axis=-1, keepdims=True)
    return jnp.where(first, lo, hi)


def _wkv_body(r_ref, lw_ref, k_ref, v_ref, a_ref, vec_ref, s0_ref, y_ref, sout_ref, state, *, chunk):
    c = chunk
    tb = r_ref.shape[0]
    n = RWKV_HEAD
    t = pl.program_id(2)
    pairs = r_ref.shape[1] // LANES

    @pl.when(t == 0)
    def _():
        state[...] = jnp.zeros_like(state)
        for p in range(pairs):
            state[p, 0:n, 0:n] = s0_ref[2 * p]
            state[p, n:, n:] = s0_ref[2 * p + 1]

    ts = min(tb, WKV_SUB)
    tri, strict, eye = _chunk_masks(ts, c)
    tri01 = _ones_where(tri)
    first = lax.broadcasted_iota(jnp.int32, (ts, LANES), 1) < n
    hr = lax.broadcasted_iota(jnp.int32, (LANES, LANES), 0)
    hc = lax.broadcasted_iota(jnp.int32, (LANES, LANES), 1)
    same_head = (hr < n) == (hc < n)
    eye_l = jnp.where(hr == hc, 1.0, 0.0)

    fc = lax.broadcasted_iota(jnp.int32, (c, LANES), 1) < n
    levels = max(1, int(math.ceil(math.log2(c))))
    lanes = [slice(p * LANES, (p + 1) * LANES) for p in range(pairs)]
    subs = [slice(t0, t0 + ts) for t0 in range(0, tb, ts)]

    ctx = [_wkv_scores(r_ref[rs, ls], lw_ref[rs, ls], k_ref[rs, ls], v_ref[rs, ls], a_ref[rs, ls], vec_ref[:, ls],
                       c, tri, tri01, strict, eye, first) for rs in subs for ls in lanes]
    for _ in range(1, levels):
        for x in ctx:
            x["nks"] = [_mm(nk, nk) for nk in x["nks"]]
        for x in ctx:
            x["t_invs"] = [t_inv + _mm(nk, t_inv) for nk, t_inv in zip(x["nks"], x["t_invs"])]
    for x in ctx:
        x["ws"] = [_mm(t_inv, rhs) for t_inv, rhs in zip(x["t_invs"], x["rhs"])]
    for x in ctx:
        _wkv_state_free_terms(x, first)

    sps = [state[p] for p in range(pairs)]
    for c0 in range(0, tb, c):
        sub, off = divmod(c0, ts)
        rows = slice(off, off + c)
        for p in range(pairs):
            x = ctx[sub * pairs + p]
            gam = jnp.exp(x["cl"][off + c - 1:off + c, :])
            gp = jnp.where(same_head, eye_l + _mm(x["wa"][rows], x["bt"][rows], TN), 0.0) * gam
            hp = jnp.where(same_head,
                           _dot(jnp.concatenate([x["u0"][rows], x["v"][rows]], axis=0).astype(BF16),
                                jnp.concatenate([x["bt"][rows], x["kt"][rows]], axis=0).astype(BF16), TN), 0.0) * gam
            y = _mm(x["rq"][rows], sps[p], NT) + x["y0"][rows]
            sps[p] = _mm(sps[p], gp) + hp
            yc = y - _pair_sum(y, fc) * (1.0 / n)
            var = _pair_sum(yc * yc, fc) * (1.0 / n)
            y_ref[c0:c0 + c, lanes[p]] = (yc * lax.rsqrt(var + RWKV_GN_EPS) * x["ln_w"] + x["ln_b"]
                                          + x["bonus"][rows])
    for p in range(pairs):
        state[p] = sps[p]

    @pl.when(t == pl.num_programs(2) - 1)
    def _():
        for p in range(pairs):
            sout_ref[2 * p] = state[p, 0:n, 0:n]
            sout_ref[2 * p + 1] = state[p, n:, n:]


def _wkv_scores(r, lw, k, v, a, vec, c, tri, tri01, strict, eye, first):
    tb = r.shape[0]
    k_k, k_a, r_k, ln_w, ln_b = vec[0:1], vec[1:2], vec[2:3], vec[3:4], vec[4:5]
    kq = k * k_k
    kkn = kq * lax.rsqrt(_pair_sum(kq * kq, first) + NORM_EPS)
    kp = k * (1.0 + (a - 1.0) * k_a)
    cl = _mask_mm2(tri01, lw)
    e_neg = jnp.exp(-cl)
    at = -kkn * jnp.exp(cl - lw)
    rt = r * jnp.exp(cl)
    bt = (kkn * a) * e_neg
    kt = kp * e_neg
    kcat = jnp.concatenate([bt, kt], axis=0).astype(BF16)
    vb = v.astype(BF16)
    x = dict(cl=cl, rt=rt, bt=bt, kt=kt, v=v, ln_w=ln_w, ln_b=ln_b,
             bonus=_pair_sum(r * kp * r_k, first) * v, arbs=[], avrs=[], nks=[], t_invs=[], rhs=[])
    for head_lanes in (first, jnp.logical_not(first)):
        at_e = jnp.where(head_lanes, at, 0.0)
        ma = _dot(at_e.astype(BF16), kcat, NT)
        mr = _dot(jnp.where(head_lanes, rt, 0.0).astype(BF16), kcat, NT)
        nk = jnp.where(strict, ma[:, :tb], 0.0)
        av_a = _dot(jnp.where(strict, ma[:, tb:], 0.0).astype(BF16), vb)
        x["avrs"].append(_dot(jnp.where(tri, mr[:, tb:], 0.0).astype(BF16), vb))
        x["arbs"].append(jnp.where(tri, mr[:, :tb], 0.0).astype(BF16))
        x["nks"].append(nk)
        x["t_invs"].append(jnp.where(eye, 1.0, nk))
        x["rhs"].append(jnp.concatenate([at_e, av_a], axis=1))
    return x


def _wkv_state_free_terms(x, first):
    ws, arbs, avrs = x["ws"], x["arbs"], x["avrs"]
    x["wa"] = ws[0][:, :LANES] + ws[1][:, :LANES]
    x["u0"] = jnp.where(first, ws[0][:, LANES:], ws[1][:, LANES:])
    wcat = jnp.concatenate([x["wa"], x["u0"]], axis=1).astype(BF16)
    aw0 = _dot(arbs[0], wcat)
    aw1 = _dot(arbs[1], wcat)
    x["rq"] = x["rt"] + jnp.where(first, aw0[:, :LANES], aw1[:, :LANES])
    x["y0"] = jnp.where(first, aw0[:, LANES:] + avrs[0], aw1[:, LANES:] + avrs[1])


def _wkv(r, lw, k, v, a, vec8, s0, chunk, tb, pairs):
    b, l, d = r.shape
    width = pairs * LANES
    heads = width // RWKV_HEAD
    row = pl.BlockSpec((None, tb, width), lambda bi, p, t: (bi, t, p))
    st = pl.BlockSpec((None, heads, RWKV_HEAD, RWKV_HEAD), lambda bi, p, t: (bi, p, 0, 0))
    return pl.pallas_call(
        functools.partial(_wkv_body, chunk=chunk),
        grid=(b, d // width, l // tb),
        in_specs=[row] * 5 + [pl.BlockSpec((SUBLANES, width), lambda bi, p, t: (0, p)), st],
        out_specs=[row, st],
        out_shape=[jax.ShapeDtypeStruct((b, l, d), F32),
                   jax.ShapeDtypeStruct((b, RWKV_HEADS, RWKV_HEAD, RWKV_HEAD), F32)],
        scratch_shapes=[pltpu.VMEM((pairs, LANES, LANES), F32)],
        compiler_params=_cparams(("arbitrary", "arbitrary", "arbitrary")),
        name="wkv7_chunked",
    )(r, lw, k, v, a, vec8, s0)


def _pad_rows(a, rows):
    return jnp.pad(a, ((0, rows - a.shape[0]), (0, 0)))


def _mods(m_rows):
    return [t[:, None, :] for t in jnp.split(m_rows, 6, axis=-1)]


def _layer0_weights(w_in, conv_w, a_log, dt_bias, fox_fbias, w_out, ff_w1, ff_w2):
    c = (0, 1536, 2048, 2052, 2056, 2568, 3080, 3592, 3596)
    seg = lambda i: w_in[:, c[i]:c[i + 1]]
    small = jnp.concatenate([seg(2), seg(3), seg(7)], axis=1)
    small = jnp.pad(small, ((0, 0), (0, LANES - small.shape[1])))
    w_all = jnp.concatenate([seg(0), seg(1), seg(4), seg(5), seg(6), small], axis=1).astype(BF16)
    aux = jnp.zeros((SUBLANES, LANES), F32)
    aux = aux.at[0, 4:8].set(a_log).at[1, 4:8].set(dt_bias).at[2, 8:12].set(fox_fbias)
    return dict(w_all=w_all, aux=aux, conv_w8=_pad_rows(conv_w, SUBLANES),
                w_out=w_out.astype(BF16), ff_w1=ff_w1.astype(BF16), ff_w2=ff_w2.astype(BF16))


def _hybrid_layer(x, mods, conv_buf, s0, fox_cache, wts, norm_mix, norm_ff, gdn_onorm, fox_qnorm, fox_knorm,
                  tm, chunk, tb, blk):
    b, l, d = x.shape
    sh1, sc1, gt1, sh2, sc2, gt2 = mods
    row = lambda g: g.reshape(1, -1)
    qkv_raw, z, small, q_b, k_f, v_f, k_b, v_b, k_a, cref = _inproj0(
        x, sh1, sc1, row(norm_mix), wts["w_all"], row(fox_qnorm), row(fox_knorm), wts["aux"], tm)
    conv_new = qkv_raw[:, l - (GDN_CONV - 1):, :]
    logf = small[:, :, 8:12]
    conv_init8 = jnp.pad(conv_buf, ((0, 0), (SUBLANES - (GDN_CONV - 1), 0), (0, 0)))
    o_a, s_new = _gdn(qkv_raw, z, small, conv_init8, s0, wts["conv_w8"], row(gdn_onorm), chunk, tb)

    if fox_cache is None:
        assert blk == tm and b == 1
        o_b = _fox_prompt(q_b[0], k_b[0], k_a[0], v_b[0], jnp.transpose(cref[0, :, 0, 8:8 + FOX_HEADS]), blk)[None]
    else:
        k_cache, v_cache, logf_cache = fox_cache
        p = k_cache.shape[1]
        total = p + l
        padded = -(-total // (SUBLANES * LANES)) * (SUBLANES * LANES)
        lf = jnp.concatenate([jnp.transpose(logf_cache, (0, 2, 1)), jnp.transpose(logf, (0, 2, 1))], axis=2)
        lf = jnp.pad(lf, ((0, 0), (0, 0), (padded - total, 0)))
        cs = _cumsum_rows(lf.reshape(b * FOX_HEADS, padded // LANES, LANES)).reshape(b, FOX_HEADS, padded)
        c_cache = cs[:, :, None, padded - total:padded - l]
        c_new = cs[:, :, padded - l:]
        o_b = _fox_step(q_b, k_f.reshape(b, l, FOX_W), v_f.reshape(b, l, FOX_W),
                        k_cache.reshape(b, p, FOX_W), v_cache.reshape(b, p, FOX_W),
                        c_cache, c_new[:, :, None, :], c_new[:, :, :, None])

    x = _mix_mlp(_mix0_mlp_body, "l0_out_mlp", x, [o_a, o_b], [gt1, sh2, sc2, gt2], row(norm_ff),
                 wts["w_out"], wts["ff_w1"], wts["ff_w2"], tm)
    return (x, conv_new, s_new, k_f, v_f, logf)


def _layer1_weights(mu, w_r, w_k, w_v, w0, w1, w2, a0, a1, a2, g1, g2, k_k, k_a, r_k, ln_w, ln_b, w_o, ff_w1, ff_w2):
    padc = lambda w: jnp.pad(w, ((0, 0), (0, -w.shape[1] % LANES))).astype(BF16)
    padr = lambda w: jnp.pad(w, ((0, -w.shape[0] % LANES), (0, 0))).astype(BF16)
    proj = [w_r.astype(BF16), w_k.astype(BF16), w_v.astype(BF16),
            padc(w1), padr(w2), padc(a1), padr(a2), padc(g1), padr(g2)]
    return dict(mu8=_pad_rows(mu, SUBLANES), pvec=_pad_rows(jnp.stack([w0, a0]), SUBLANES),
                svec=_pad_rows(jnp.stack([k_k, k_a, r_k.reshape(-1), ln_w, ln_b]), SUBLANES),
                proj=proj, w_o=w_o.astype(BF16), ff_w1=ff_w1.astype(BF16), ff_w2=ff_w2.astype(BF16))


def _rwkv_layer(x, mods, shift_prev, s0, wts, norm_mix, norm_ff, tm, chunk, tb, pairs):
    sh1, sc1, gt1, sh2, sc2, gt2 = mods
    row = lambda g: g.reshape(1, -1)
    r, lw, k, v, a, gate, shift_new = _rwkv_proj(
        x, shift_prev[:, None, :], sh1, sc1, row(norm_mix), wts["mu8"], wts["pvec"], wts["proj"], tm)
    y, s_new = _wkv(r, lw, k, v, a, wts["svec"], s0, chunk, tb, pairs)
    x = _mix_mlp(_mix1_mlp_body, "l1_out_mlp", x, [y, gate], [gt1, sh2, sc2, gt2], row(norm_ff),
                 wts["w_o"], wts["ff_w1"], wts["ff_w2"], tm)
    return x, shift_new[:, 0, :], s_new


def kernel(x_prompt, x_sample, c_prompt, c_sample, cache_l0_conv, state_l0_delta, cache_l0_fox_k, cache_l0_fox_v, cache_l0_fox_logf, state_l1_shift, state_l1_wkv, l0_ada_w, l0_ada_b, l0_norm_mix, l0_norm_ff, l0_w_in, l0_conv_w, l0_a_log, l0_dt_bias, l0_gdn_onorm, l0_fox_qnorm, l0_fox_knorm, l0_fox_fbias, l0_w_out, l0_ff_w1, l0_ff_w2, l1_ada_w, l1_ada_b, l1_norm_mix, l1_norm_ff, l1_mu, l1_w_r, l1_w_k, l1_w_v, l1_w0, l1_w1, l1_w2, l1_a0, l1_a1, l1_a2, l1_g1, l1_g2, l1_k_k, l1_k_a, l1_r_k, l1_ln_w, l1_ln_b, l1_w_o, l1_ff_w1, l1_ff_w2):
    bp, lp, d = x_prompt.shape
    bs, ls, _ = x_sample.shape
    c_all = _pad_rows(jnp.concatenate([c_prompt, c_sample], axis=0), 2 * SUBLANES)
    m0 = _ada(c_all, l0_ada_w, l0_ada_b)
    m1 = _ada(c_all, l1_ada_w, l1_ada_b)
    mods0_p, mods0_s = _mods(m0[:bp]), _mods(m0[bp:bp + bs])
    mods1_p, mods1_s = _mods(m1[:bp]), _mods(m1[bp:bp + bs])

    w0 = _layer0_weights(l0_w_in, l0_conv_w, l0_a_log, l0_dt_bias, l0_fox_fbias, l0_w_out, l0_ff_w1, l0_ff_w2)
    w1 = _layer1_weights(l1_mu, l1_w_r, l1_w_k, l1_w_v, l1_w0, l1_w1, l1_w2, l1_a0, l1_a1, l1_a2, l1_g1, l1_g2,
                         l1_k_k, l1_k_a, l1_r_k, l1_ln_w, l1_ln_b, l1_w_o, l1_ff_w1, l1_ff_w2)
    norms0 = (l0_norm_mix, l0_norm_ff, l0_gdn_onorm, l0_fox_qnorm, l0_fox_knorm)

    tm_p = min(512, lp)
    y_p, conv_p, delta_p, fox_k_p, fox_v_p, fox_logf_p = _hybrid_layer(
        x_prompt, mods0_p, jnp.zeros((bp, GDN_CONV - 1, GDN_QKV), F32),
        jnp.zeros((bp, GDN_HEADS, GDN_DK, GDN_DV), F32), None, w0, *norms0,
        tm=tm_p, chunk=min(GDN_CHUNK, lp), tb=min(256, lp), blk=min(512, lp))
    y_s, conv_s, delta_s, fox_k_s, fox_v_s, fox_logf_s = _hybrid_layer(
        x_sample, mods0_s, cache_l0_conv, state_l0_delta,
        (cache_l0_fox_k, cache_l0_fox_v, cache_l0_fox_logf), w0, *norms0,
        tm=ls, chunk=min(GDN_CHUNK, ls), tb=ls, blk=None)

    y_p, shift_p, wkv_p = _rwkv_layer(
        y_p, mods1_p, jnp.zeros((bp, d), F32), jnp.zeros((bp, RWKV_HEADS, RWKV_HEAD, RWKV_HEAD), F32),
        w1, l1_norm_mix, l1_norm_ff, tm=tm_p, chunk=min(RWKV_CHUNK, lp), tb=min(512, lp), pairs=WKV_PAIRS)
    y_s, shift_s, wkv_s = _rwkv_layer(
        y_s, mods1_s, state_l1_shift, state_l1_wkv, w1, l1_norm_mix, l1_norm_ff,
        tm=ls, chunk=min(RWKV_CHUNK, ls), tb=ls, pairs=d // LANES)

    return (y_p, y_s, conv_p, conv_s, delta_p, delta_s, fox_k_p, fox_k_s,
            fox_v_p, fox_v_s, fox_logf_p, fox_logf_s, shift_p, shift_s, wkv_p, wkv_s)
```
